```python
import math
import jax
import jax.numpy as jnp
from jax import lax
import numpy as np

D_MODEL = 1024
BATCH = 8
SEQ = 2048
DEPTH = 4
DEC_BATCH = 128
DEC_SEQ = 1
PAST_LEN = 8192
PAGE_SIZE = 128

HEAD_DIM = 64
FOX_HEADS = 8
FOX_KV_HEADS = 2
DSA_HEADS = 8
IDX_HEADS = 4
IDX_DIM = 64
IDX_TOPK = 256
MLA_HEADS = 8
MLA_NOPE_DIM = 64
MLA_ROPE_DIM = 32
MLA_V_DIM = 64
MLA_KV_RANK = 128
MLA_SCALE = (MLA_NOPE_DIM + MLA_ROPE_DIM) ** -0.5
ROPE_BASE = 10000.0
NUM_BUCKETS = 32
MAX_DISTANCE = 128
D_FF = 4 * D_MODEL
N_BRANCHES = 3
Q_BLOCK = 128
EPS = 1e-6

FOX_WIDTH = FOX_HEADS * HEAD_DIM
DSA_WIDTH = DSA_HEADS * HEAD_DIM
MLA_WIDTH = MLA_HEADS * MLA_V_DIM

IN_SPLITS = (
    ('fox_q', FOX_HEADS * HEAD_DIM),
    ('fox_k', FOX_KV_HEADS * HEAD_DIM),
    ('fox_v', FOX_KV_HEADS * HEAD_DIM),
    ('fox_f', FOX_HEADS),
    ('dsa_q', DSA_HEADS * HEAD_DIM),
    ('dsa_k', HEAD_DIM),
    ('dsa_v', HEAD_DIM),
    ('idx_q', IDX_HEADS * IDX_DIM),
    ('idx_k', IDX_DIM),
    ('idx_w', IDX_HEADS),
    ('mla_q', MLA_HEADS * (MLA_NOPE_DIM + MLA_ROPE_DIM)),
    ('mla_ckv', MLA_KV_RANK),
    ('mla_kr', MLA_ROPE_DIM),
    ('gates', N_BRANCHES * D_MODEL),
)
D_IN = sum(width for _, width in IN_SPLITS)

kernel_name = 'hybrid_fox_dsa_mla_step'


def split_in(z):
    parts = {}
    off = 0
    for name, width in IN_SPLITS:
        parts[name] = z[..., off:off + width]
        off += width
    return parts


def rmsnorm(x, g):
    xf = x.astype(jnp.float32)
    y = xf * lax.rsqrt(jnp.mean(xf * xf, axis=-1, keepdims=True) + EPS)
    return (y * g.astype(jnp.float32)).astype(x.dtype)


def apply_rope(x, pos):
    half = x.shape[-1] // 2
    inv = ROPE_BASE ** (-jnp.arange(half, dtype=jnp.float32) / half)
    ang = pos.astype(jnp.float32)[:, None] * inv[None, :]
    shape = (pos.shape[0],) + (1,) * (x.ndim - 3) + (half,)
    cos = jnp.cos(ang).reshape(shape)
    sin = jnp.sin(ang).reshape(shape)
    xf = x.astype(jnp.float32)
    x1, x2 = xf[..., :half], xf[..., half:]
    return jnp.concatenate([x1 * cos - x2 * sin, x2 * cos + x1 * sin], axis=-1).astype(x.dtype)


def t5_bucket(dist):
    max_exact = NUM_BUCKETS // 2
    d = jnp.maximum(dist, 0)
    log_ratio = jnp.log(jnp.maximum(d, max_exact).astype(jnp.float32) / max_exact) / math.log(MAX_DISTANCE / max_exact)
    large = jnp.minimum(max_exact + (log_ratio * (NUM_BUCKETS - max_exact)).astype(jnp.int32), NUM_BUCKETS - 1)
    return jnp.where(d < max_exact, d, large)


def sweep_query_blocks(fn, qpos, *q_arrays):
    T = qpos.shape[0]
    qb = Q_BLOCK if T % Q_BLOCK == 0 else T
    nb = T // qb

    def to_blocks(a):
        return jnp.moveaxis(a.reshape((a.shape[0], nb, qb) + a.shape[2:]), 1, 0)

    blocks = (qpos.reshape(nb, qb),) + tuple(to_blocks(a) for a in q_arrays)
    out = lax.map(lambda args: fn(*args), blocks)
    out = jnp.moveaxis(out, 0, 1)
    return out.reshape((out.shape[0], T) + out.shape[3:])


def fox_block(qpos, q, cq, k, v, ck):
    B, qb, H, D = q.shape
    L = k.shape[1]
    G = H // FOX_KV_HEADS
    s = jnp.einsum('bqkgd,bskd->bkgqs', q.reshape(B, qb, FOX_KV_HEADS, G, D), k,
                   preferred_element_type=jnp.float32) * (D ** -0.5)
    decay = (jnp.transpose(cq.reshape(B, qb, FOX_KV_HEADS, G), (0, 2, 3, 1))[..., :, None]
             - jnp.transpose(ck.reshape(B, L, FOX_KV_HEADS, G), (0, 2, 3, 1))[..., None, :])
    causal = jnp.arange(L)[None, :] <= qpos[:, None]
    s = jnp.where(causal, s + decay, -jnp.inf)
    p = jax.nn.softmax(s, axis=-1).astype(v.dtype)
    o = jnp.einsum('bkgqs,bskd->bqkgd', p, v)
    return o.reshape(B, qb, H * D)


def dsa_block(qpos, q, iq, iw, k, v, ik, rel_bias, topk):
    B, qb, H, D = q.shape
    L = k.shape[1]
    causal = jnp.arange(L)[None, :] <= qpos[:, None]
    dots = jnp.einsum('bqhd,bsd->bqhs', iq, ik, preferred_element_type=jnp.float32) * (IDX_DIM ** -0.5)
    score = jnp.einsum('bqhs,bqh->bqs', jax.nn.relu(dots), iw.astype(jnp.float32) * (IDX_HEADS ** -0.5))
    score = jnp.where(causal[None], score, -jnp.inf)
    _, idx = lax.top_k(score, topk)
    flat = idx.reshape(B, qb * topk)[..., None]
    ks = jnp.take_along_axis(k, flat, axis=1).reshape(B, qb, topk, D)
    vs = jnp.take_along_axis(v, flat, axis=1).reshape(B, qb, topk, D)
    dist = qpos[None, :, None] - idx
    s = jnp.einsum('bqhd,bqkd->bqhk', q, ks, preferred_element_type=jnp.float32) * (D ** -0.5)
    s = s + jnp.moveaxis(rel_bias[t5_bucket(dist)], -1, 2).astype(jnp.float32)
    s = jnp.where((dist >= 0)[:, :, None, :], s, -jnp.inf)
    p = jax.nn.softmax(s, axis=-1).astype(vs.dtype)
    o = jnp.einsum('bqhk,bqkd->bqhd', p, vs)
    return o.reshape(B, qb, H * D)


def mla_block(qpos, q_lat, q_rope, c_all, kr_all):
    L = c_all.shape[1]
    s = (jnp.einsum('bqhr,bsr->bhqs', q_lat, c_all, preferred_element_type=jnp.float32)
         + jnp.einsum('bqhe,bse->bhqs', q_rope, kr_all, preferred_element_type=jnp.float32)) * MLA_SCALE
    causal = jnp.arange(L)[None, :] <= qpos[:, None]
    s = jnp.where(causal, s, -jnp.inf)
    p = jax.nn.softmax(s, axis=-1).astype(c_all.dtype)
    return jnp.einsum('bhqs,bsr->bqhr', p, c_all)


def token_mixers(xn, pos, past, lp, rel_bias):
    B, T, _ = xn.shape
    f32 = jnp.float32
    z = split_in(jnp.einsum('btd,de->bte', xn, lp['w_in']))
    fox_k = z['fox_k'].reshape(B, T, FOX_KV_HEADS, HEAD_DIM)
    fox_v = z['fox_v'].reshape(B, T, FOX_KV_HEADS, HEAD_DIM)
    logf = jax.nn.log_sigmoid(z['fox_f'].astype(f32) + lp['b_forget'].astype(f32))
    ckv = rmsnorm(z['mla_ckv'], lp['g_mla_kv'])
    k_rope = apply_rope(z['mla_kr'], pos)
    new_rows = (jnp.stack([fox_k, fox_v], axis=2),
                logf,
                jnp.stack([z['dsa_k'], z['dsa_v'], z['idx_k']], axis=2),
                jnp.concatenate([ckv, k_rope], axis=-1))
    if past is None:
        full = new_rows
    else:
        full = tuple(jnp.concatenate([p_rows, n_rows], axis=1) for p_rows, n_rows in zip(past, new_rows))
    all_fox_kv, all_logf, all_dsa, all_mla = full
    L = all_fox_kv.shape[1]

    cum = jnp.cumsum(all_logf.astype(f32), axis=1)
    fox_q = z['fox_q'].reshape(B, T, FOX_HEADS, HEAD_DIM)
    kf, vf = all_fox_kv[:, :, 0], all_fox_kv[:, :, 1]
    o_fox = sweep_query_blocks(lambda qp, q, c: fox_block(qp, q, c, kf, vf, cum),
                               pos, fox_q, cum[:, L - T:])

    topk = min(IDX_TOPK, L // 4)
    dsa_q = z['dsa_q'].reshape(B, T, DSA_HEADS, HEAD_DIM)
    idx_q = z['idx_q'].reshape(B, T, IDX_HEADS, IDX_DIM)
    kd, vd, kidx = all_dsa[:, :, 0], all_dsa[:, :, 1], all_dsa[:, :, 2]
    o_dsa = sweep_query_blocks(lambda qp, q, iq, iw: dsa_block(qp, q, iq, iw, kd, vd, kidx, rel_bias, topk),
                               pos, dsa_q, idx_q, z['idx_w'])

    mq = z['mla_q'].reshape(B, T, MLA_HEADS, MLA_NOPE_DIM + MLA_ROPE_DIM)
    q_lat = jnp.einsum('bthd,rhd->bthr', mq[..., :MLA_NOPE_DIM], lp['w_mla_uk'])
    q_rope = apply_rope(mq[..., MLA_NOPE_DIM:], pos)
    c_all, kr_all = all_mla[..., :MLA_KV_RANK], all_mla[..., MLA_KV_RANK:]
    o_lat = sweep_query_blocks(lambda qp, ql, qr: mla_block(qp, ql, qr, c_all, kr_all),
                               pos, q_lat, q_rope)
    o_mla = jnp.einsum('bthr,rhd->bthd', o_lat, lp['w_mla_uv']).reshape(B, T, MLA_WIDTH)

    gates = jax.nn.sigmoid(z['gates'].astype(f32)).astype(xn.dtype).reshape(B, T, N_BRANCHES, D_MODEL)
    merged = (gates[:, :, 0] * jnp.einsum('bte,ed->btd', o_fox, lp['w_branch_fox'])
              + gates[:, :, 1] * jnp.einsum('bte,ed->btd', o_dsa, lp['w_branch_dsa'])
              + gates[:, :, 2] * jnp.einsum('bte,ed->btd', o_mla, lp['w_branch_mla']))
    return jnp.einsum('btd,de->bte', merged, lp['w_out']), new_rows


def squared_relu_mlp(x, w_up, w_down):
    h = jax.nn.relu(jnp.einsum('btd,df->btf', x, w_up))
    return jnp.einsum('btf,fd->btd', h * h, w_down)


def gather_pages(cache_l, page_table):
    g = cache_l[page_table]
    return g.reshape((page_table.shape[0], page_table.shape[1] * PAGE_SIZE) + cache_l.shape[2:])


def run_trunk(x, caches, page_table, layer_params, rel_bias, g_final):
    T = x.shape[1]
    past_len = 0 if page_table is None else page_table.shape[1] * PAGE_SIZE
    pos = past_len + jnp.arange(T, dtype=jnp.int32)
    rows = ([], [], [], [])
    for l in range(DEPTH):
        lp = {name: arr[l] for name, arr in layer_params.items()}
        past = None if page_table is None else tuple(gather_pages(c[l], page_table) for c in caches)
        mix, new_rows = token_mixers(rmsnorm(x, lp['g_attn']), pos, past, lp, rel_bias)
        x = x + mix
        x = x + squared_relu_mlp(rmsnorm(x, lp['g_mlp']), lp['w_up'], lp['w_down'])
        for acc, r in zip(rows, new_rows):
            acc.append(r)
    return rmsnorm(x, g_final), tuple(jnp.stack(acc, axis=0) for acc in rows)


def setup_inputs(seed: int = 0) -> dict:
    key = jax.random.key(seed)
    ks = jax.random.split(key, 24)
    f32 = jnp.float32

    def nrm(k, shape, scale):
        return jax.random.normal(k, shape, f32) * scale

    n_pages = PAST_LEN // PAGE_SIZE
    n_used = DEC_BATCH * n_pages
    n_pool = n_used + n_used // 4
    row_shape_logf = (DEPTH, n_pool, PAGE_SIZE, FOX_HEADS)
    return {
        'x_prompt': nrm(ks[0], (BATCH, SEQ, D_MODEL), 1.0),
        'x_sample': nrm(ks[1], (DEC_BATCH, DEC_SEQ, D_MODEL), 1.0),
        'cache_fox_kv': nrm(ks[2], (DEPTH, n_pool, PAGE_SIZE, 2, FOX_KV_HEADS, HEAD_DIM), 1.0),
        'cache_fox_logf': jax.nn.log_sigmoid(jax.random.uniform(ks[3], row_shape_logf, f32, 1.0, 6.0)
                                             + nrm(ks[4], row_shape_logf, 1.0)),
        'cache_dsa_kv': nrm(ks[5], (DEPTH, n_pool, PAGE_SIZE, 3, HEAD_DIM), 1.0),
        'cache_mla': nrm(ks[6], (DEPTH, n_pool, PAGE_SIZE, MLA_KV_RANK + MLA_ROPE_DIM), 1.0),
        'page_table': jax.random.permutation(ks[7], n_pool)[:n_used].reshape(DEC_BATCH, n_pages).astype(jnp.int32),
        'g_attn': 1.0 + nrm(ks[8], (DEPTH, D_MODEL), 0.05),
        'w_in': nrm(ks[9], (DEPTH, D_MODEL, D_IN), D_MODEL ** -0.5),
        'b_forget': jax.random.uniform(ks[10], (DEPTH, FOX_HEADS), f32, 1.0, 6.0),
        'g_mla_kv': 1.0 + nrm(ks[11], (DEPTH, MLA_KV_RANK), 0.05),
        'w_mla_uk': nrm(ks[12], (DEPTH, MLA_KV_RANK, MLA_HEADS, MLA_NOPE_DIM), MLA_KV_RANK ** -0.5),
        'w_mla_uv': nrm(ks[13], (DEPTH, MLA_KV_RANK, MLA_HEADS, MLA_V_DIM), MLA_KV_RANK ** -0.5),
        'w_branch_fox': nrm(ks[14], (DEPTH, FOX_WIDTH, D_MODEL), FOX_WIDTH ** -0.5),
        'w_branch_dsa': nrm(ks[15], (DEPTH, DSA_WIDTH, D_MODEL), DSA_WIDTH ** -0.5),
        'w_branch_mla': nrm(ks[16], (DEPTH, MLA_WIDTH, D_MODEL), MLA_WIDTH ** -0.5),
        'w_out': nrm(ks[17], (DEPTH, D_MODEL, D_MODEL), D_MODEL ** -0.5),
        'g_mlp': 1.0 + nrm(ks[18], (DEPTH, D_MODEL), 0.05),
        'w_up': nrm(ks[19], (DEPTH, D_MODEL, D_FF), D_MODEL ** -0.5),
        'w_down': nrm(ks[20], (DEPTH, D_FF, D_MODEL), D_FF ** -0.5),
        'rel_bias': nrm(ks[21], (NUM_BUCKETS, DSA_HEADS), 0.5),
        'g_final': 1.0 + nrm(ks[22], (D_MODEL,), 0.05),
    }


def reference(x_prompt, x_sample, cache_fox_kv, cache_fox_logf, cache_dsa_kv, cache_mla, page_table,
              g_attn, w_in, b_forget, g_mla_kv, w_mla_uk, w_mla_uv, w_branch_fox, w_branch_dsa,
              w_branch_mla, w_out, g_mlp, w_up, w_down, rel_bias, g_final):
    layer_params = {
        'g_attn': g_attn, 'w_in': w_in, 'b_forget': b_forget, 'g_mla_kv': g_mla_kv,
        'w_mla_uk': w_mla_uk, 'w_mla_uv': w_mla_uv, 'w_branch_fox': w_branch_fox,
        'w_branch_dsa': w_branch_dsa, 'w_branch_mla': w_branch_mla, 'w_out': w_out,
        'g_mlp': g_mlp, 'w_up': w_up, 'w_down': w_down,
    }
    y_prompt, (fox_kv_p, fox_logf_p, dsa_kv_p, mla_p) = run_trunk(
        x_prompt, None, None, layer_params, rel_bias, g_final)
    y_sample, (fox_kv_s, fox_logf_s, dsa_kv_s, mla_s) = run_trunk(
        x_sample, (cache_fox_kv, cache_fox_logf, cache_dsa_kv, cache_mla), page_table,
        layer_params, rel_bias, g_final)
    return (y_prompt, y_sample, fox_kv_p, fox_logf_p, dsa_kv_p, mla_p, fox_kv_s, fox_logf_s, dsa_kv_s, mla_s)
```

```python
import functools
import math

import jax
import jax.numpy as jnp
from jax import lax
from jax.experimental import pallas as pl
from jax.experimental.pallas import tpu as pltpu

F32 = jnp.float32
BF16 = jnp.bfloat16
I32 = jnp.int32
HIGHEST = lax.Precision.HIGHEST

D_MODEL = 1024
DEPTH = 4
PAGE = 128
HEAD_DIM = 64
N_HEADS = 8
FOX_KV_HEADS = 2
IDX_HEADS = 4
IDX_TOPK = 256
MLA_NOPE = 64
MLA_ROPE = 32
MLA_RANK = 128
MLA_SCALE = (MLA_NOPE + MLA_ROPE) ** -0.5
ROPE_BASE = 10000.0
NUM_BUCKETS = 32
MAX_DISTANCE = 128
D_FF = 4 * D_MODEL
EPS = 1e-6
ATT_SCALE = HEAD_DIM ** -0.5
IDX_SCALE = 64 ** -0.5
IDX_W_SCALE = IDX_HEADS ** -0.5

NEG = -1e30
INT_MIN = -(2 ** 31)

Z_WIDTH = 6144
Z_SMALL = 768
C_FOXKV, C_DSA, C_CKV, C_KR, C_FOXF, C_IDXW = 0, 256, 512, 640, 672, 680
C_MLAQ = 768
C_FOXQ = 1536
C_DSAQ = 2048
C_IDXQ = 2560
C_GATES = 3072

VMEM_LIMIT = 52 * 1024 * 1024

TQ = 256
TK = 256
PG = 8


def _cparams(n_axes):
    return pltpu.CompilerParams(dimension_semantics=("arbitrary",) * n_axes, vmem_limit_bytes=VMEM_LIMIT)


def _rms_mm_body(x_ref, g_ref, w_ref, o_ref, xn_ref):
    @pl.when(pl.program_id(1) == 0)
    def _():
        x = x_ref[...]
        y = x * lax.rsqrt(jnp.mean(x * x, axis=-1, keepdims=True) + EPS)
        xn_ref[...] = (y * g_ref[...]).astype(BF16)

    o_ref[...] = jnp.dot(xn_ref[...], w_ref[...], preferred_element_type=F32)


def rms_matmul(x, g, w, tm, tn):
    n, d = x.shape
    e = w.shape[1]
    return pl.pallas_call(
        _rms_mm_body,
        grid=(n // tm, e // tn),
        in_specs=[
            pl.BlockSpec((tm, d), lambda i, j: (i, 0)),
            pl.BlockSpec((1, d), lambda i, j: (0, 0)),
            pl.BlockSpec((d, tn), lambda i, j: (0, j)),
        ],
        out_specs=pl.BlockSpec((tm, tn), lambda i, j: (i, j)),
        out_shape=jax.ShapeDtypeStruct((n, e), F32),
        scratch_shapes=[pltpu.VMEM((tm, d), BF16)],
        compiler_params=_cparams(2),
        name="rms_matmul",
    )(x, g, w)


def _post_body(z_ref, g_ref, b_ref, cos_ref, sin_ref, fox_ref, logf_ref, dsa_ref, mla_ref):
    fox_ref[...] = z_ref[:, C_FOXKV:C_FOXKV + 256]
    dsa_ref[...] = z_ref[:, C_DSA:C_DSA + 192]
    ckv = z_ref[:, C_CKV:C_CKV + MLA_RANK]
    ckvn = ckv * lax.rsqrt(jnp.mean(ckv * ckv, axis=-1, keepdims=True) + EPS) * g_ref[...]
    kr = z_ref[:, C_KR:C_KR + MLA_ROPE]
    half = MLA_ROPE // 2
    rot = jnp.concatenate([kr[:, half:], kr[:, :half]], axis=1)
    krr = kr * cos_ref[...] + rot * sin_ref[...]
    mla_ref[...] = jnp.concatenate([ckvn, krr], axis=1)
    logf_ref[...] = jax.nn.log_sigmoid(z_ref[:, C_FOXF:C_FOXF + N_HEADS] + b_ref[...])


def post_rows(z, g_mla, b_forget, cos32, sin32, tm):
    n = z.shape[0]
    nt = cos32.shape[0] // tm
    row = lambda i: (i, 0)
    tab = lambda i: (i % nt, 0)
    const = lambda i: (0, 0)
    return pl.pallas_call(
        _post_body,
        grid=(n // tm,),
        in_specs=[
            pl.BlockSpec((tm, Z_SMALL), row),
            pl.BlockSpec((1, MLA_RANK), const),
            pl.BlockSpec((1, N_HEADS), const),
            pl.BlockSpec((tm, MLA_ROPE), tab),
            pl.BlockSpec((tm, MLA_ROPE), tab),
        ],
        out_specs=[
            pl.BlockSpec((tm, 256), row),
            pl.BlockSpec((tm, N_HEADS), row),
            pl.BlockSpec((tm, 192), row),
            pl.BlockSpec((tm, MLA_RANK + MLA_ROPE), row),
        ],
        out_shape=[
            jax.ShapeDtypeStruct((n, 256), F32),
            jax.ShapeDtypeStruct((n, N_HEADS), F32),
            jax.ShapeDtypeStruct((n, 192), F32),
            jax.ShapeDtypeStruct((n, MLA_RANK + MLA_ROPE), F32),
        ],
        compiler_params=_cparams(1),
        name="post_rows",
    )(z, g_mla, b_forget, cos32, sin32)


MLAQ_SLOT = 256
MLAQ_DIM = MLA_RANK + MLA_ROPE


def _mla_q_body(z_ref, cos_ref, sin_ref, wuk_ref, o_ref):
    r1 = z_ref[:, 512:640]
    r2 = z_ref[:, 640:768]
    c = cos_ref[...]
    s = sin_ref[...]
    o1 = r1 * c - r2 * s
    o2 = r2 * c + r1 * s
    half = MLA_ROPE // 2
    tm = o1.shape[0]
    for h in range(N_HEADS):
        nope = z_ref[:, h * MLA_NOPE:(h + 1) * MLA_NOPE].astype(BF16)
        q_lat = jnp.dot(nope, wuk_ref[h], preferred_element_type=F32)
        base = h * MLAQ_SLOT
        o_ref[:, base:base + MLA_RANK] = q_lat.astype(BF16)
        q_rope = jnp.concatenate([o1[:, h * half:(h + 1) * half], o2[:, h * half:(h + 1) * half]], axis=1)
        o_ref[:, base + MLA_RANK:base + MLAQ_DIM] = q_rope.astype(BF16)
        o_ref[:, base + MLAQ_DIM:base + MLAQ_SLOT] = jnp.zeros((tm, MLAQ_SLOT - MLAQ_DIM), BF16)


def mla_q_prep(z, cos128, sin128, wuk, tm):
    n = z.shape[0]
    nt = cos128.shape[0] // tm
    tab = lambda i: (i % nt, 0)
    return pl.pallas_call(
        _mla_q_body,
        grid=(n // tm,),
        in_specs=[
            pl.BlockSpec((tm, 768), lambda i: (i, C_MLAQ // 768)),
            pl.BlockSpec((tm, 128), tab),
            pl.BlockSpec((tm, 128), tab),
            pl.BlockSpec((N_HEADS, MLA_NOPE, MLA_RANK), lambda i: (0, 0, 0)),
        ],
        out_specs=pl.BlockSpec((tm, N_HEADS * MLAQ_SLOT), lambda i: (i, 0)),
        out_shape=jax.ShapeDtypeStruct((n, N_HEADS * MLAQ_SLOT), BF16),
        compiler_params=_cparams(1),
        name="mla_q_prep",
    )(z, cos128, sin128, wuk)


def _mla_uv_body(o_ref, wuv_ref, out_ref):
    outs = []
    for h in range(N_HEADS):
        outs.append(jnp.dot(o_ref[:, h * MLA_RANK:(h + 1) * MLA_RANK], wuv_ref[h], preferred_element_type=F32))
    out_ref[...] = jnp.concatenate(outs, axis=1).astype(BF16)


def mla_uv(o_lat, wuv, tm):
    n = o_lat.shape[0]
    return pl.pallas_call(
        _mla_uv_body,
        grid=(n // tm,),
        in_specs=[
            pl.BlockSpec((tm, N_HEADS * MLA_RANK), lambda i: (i, 0)),
            pl.BlockSpec((N_HEADS, MLA_RANK, HEAD_DIM), lambda i: (0, 0, 0)),
        ],
        out_specs=pl.BlockSpec((tm, N_HEADS * HEAD_DIM), lambda i: (i, 0)),
        out_shape=jax.ShapeDtypeStruct((n, N_HEADS * HEAD_DIM), BF16),
        compiler_params=_cparams(1),
        name="mla_uv",
    )(o_lat, wuv)


def _merge_body(of_ref, od_ref, om_ref, g0_ref, g1_ref, g2_ref, wf_ref, wd_ref, wm_ref, wo_ref, x_ref, out_ref):
    merged = (jax.nn.sigmoid(g0_ref[...]) * jnp.dot(of_ref[...], wf_ref[...], preferred_element_type=F32)
              + jax.nn.sigmoid(g1_ref[...]) * jnp.dot(od_ref[...], wd_ref[...], preferred_element_type=F32)
              + jax.nn.sigmoid(g2_ref[...]) * jnp.dot(om_ref[...], wm_ref[...], preferred_element_type=F32))
    out_ref[...] = x_ref[...] + jnp.dot(merged.astype(BF16), wo_ref[...], preferred_element_type=F32)


def merge_out(o_fox, o_dsa, o_mla, z, wf, wd, wm, wo, x, tm):
    n = x.shape[0]
    row = lambda i: (i, 0)
    const = lambda i: (0, 0)
    gb = C_GATES // D_MODEL
    return pl.pallas_call(
        _merge_body,
        grid=(n // tm,),
        in_specs=[
            pl.BlockSpec((tm, 512), row),
            pl.BlockSpec((tm, 512), row),
            pl.BlockSpec((tm, 512), row),
            pl.BlockSpec((tm, D_MODEL), lambda i: (i, gb)),
            pl.BlockSpec((tm, D_MODEL), lambda i: (i, gb + 1)),
            pl.BlockSpec((tm, D_MODEL), lambda i: (i, gb + 2)),
            pl.BlockSpec((512, D_MODEL), const),
            pl.BlockSpec((512, D_MODEL), const),
            pl.BlockSpec((512, D_MODEL), const),
            pl.BlockSpec((D_MODEL, D_MODEL), const),
            pl.BlockSpec((tm, D_MODEL), row),
        ],
        out_specs=pl.BlockSpec((tm, D_MODEL), row),
        out_shape=jax.ShapeDtypeStruct((n, D_MODEL), F32),
        compiler_params=_cparams(1),
        name="merge_out",
    )(o_fox, o_dsa, o_mla, z, z, z, wf, wd, wm, wo, x)


def _mlp_body(x_ref, g_ref, wu_ref, wd_ref, o_ref, xn_ref, acc_ref):
    f = pl.program_id(1)

    @pl.when(f == 0)
    def _():
        x = x_ref[...]
        y = x * lax.rsqrt(jnp.mean(x * x, axis=-1, keepdims=True) + EPS)
        xn_ref[...] = (y * g_ref[...]).astype(BF16)
        acc_ref[...] = x

    h = jnp.maximum(jnp.dot(xn_ref[...], wu_ref[...], preferred_element_type=F32), 0.0)
    acc_ref[...] += jnp.dot((h * h).astype(BF16), wd_ref[...], preferred_element_type=F32)

    @pl.when(f == pl.num_programs(1) - 1)
    def _():
        o_ref[...] = acc_ref[...]


def mlp_residual(x, g, w_up, w_down, tm, tf):
    n, d = x.shape
    dff = w_up.shape[1]
    return pl.pallas_call(
        _mlp_body,
        grid=(n // tm, dff // tf),
        in_specs=[
            pl.BlockSpec((tm, d), lambda i, f: (i, 0)),
            pl.BlockSpec((1, d), lambda i, f: (0, 0)),
            pl.BlockSpec((d, tf), lambda i, f: (0, f)),
            pl.BlockSpec((tf, d), lambda i, f: (f, 0)),
        ],
        out_specs=pl.BlockSpec((tm, d), lambda i, f: (i, 0)),
        out_shape=jax.ShapeDtypeStruct((n, d), F32),
        scratch_shapes=[pltpu.VMEM((tm, d), BF16), pltpu.VMEM((tm, d), F32)],
        compiler_params=_cparams(2),
        name="mlp_residual",
    )(x, g, w_up, w_down)


def _rms_body(x_ref, g_ref, o_ref):
    x = x_ref[...]
    y = x * lax.rsqrt(jnp.mean(x * x, axis=-1, keepdims=True) + EPS)
    o_ref[...] = y * g_ref[...]


def rms_only(x, g, tm):
    n, d = x.shape
    return pl.pallas_call(
        _rms_body,
        grid=(n // tm,),
        in_specs=[pl.BlockSpec((tm, d), lambda i: (i, 0)), pl.BlockSpec((1, d), lambda i: (0, 0))],
        out_specs=pl.BlockSpec((tm, d), lambda i: (i, 0)),
        out_shape=jax.ShapeDtypeStruct((n, d), F32),
        compiler_params=_cparams(1),
        name="rms_only",
    )(x, g)


def _tri_incl(n):
    r = lax.broadcasted_iota(I32, (n, n), 0)
    c = lax.broadcasted_iota(I32, (n, n), 1)
    return (c <= r).astype(F32)


def _cumsum_body(lf_ref, cum_ref, cumt_ref):
    t = lf_ref.shape[1]
    tri = _tri_incl(TK)
    carry = jnp.zeros((1, N_HEADS), F32)
    for c in range(t // TK):
        lf = lf_ref[0, c * TK:(c + 1) * TK, :]
        cum = jnp.dot(tri, lf, precision=HIGHEST, preferred_element_type=F32) + carry
        carry = cum[TK - 1:TK, :]
        cum_ref[0, c * TK:(c + 1) * TK, :] = cum
        cumt_ref[0, c] = cum.T


def fox_cumsum(logf):
    b, t, h = logf.shape
    return pl.pallas_call(
        _cumsum_body,
        grid=(b,),
        in_specs=[pl.BlockSpec((1, t, h), lambda i: (i, 0, 0))],
        out_specs=[
            pl.BlockSpec((1, t, h), lambda i: (i, 0, 0)),
            pl.BlockSpec((1, t // TK, h, TK), lambda i: (i, 0, 0, 0)),
        ],
        out_shape=[
            jax.ShapeDtypeStruct((b, t, h), F32),
            jax.ShapeDtypeStruct((b, t // TK, h, TK), F32),
        ],
        compiler_params=_cparams(1),
        name="fox_cumsum",
    )(logf)


def _online_update(s, v, m, l, acc):
    m_new = jnp.maximum(m, jnp.max(s, axis=-1, keepdims=True))
    alpha = jnp.exp(m - m_new)
    p = jnp.exp(s - m_new)
    l_new = alpha * l + jnp.sum(p, axis=-1, keepdims=True)
    acc_new = alpha * acc + jnp.dot(p.astype(BF16), v, preferred_element_type=F32)
    return m_new, l_new, acc_new


def _nt_dot(a, b):
    return lax.dot_general(a, b, (((1,), (1,)), ((), ())), preferred_element_type=F32)


def _fox_prompt_body(q_ref, kv_ref, cq_ref, ck_ref, o_ref):
    i = pl.program_id(1)
    rows = i * TQ + lax.broadcasted_iota(I32, (TQ, 1), 0)
    group = N_HEADS // FOX_KV_HEADS
    outs = []
    for h in range(N_HEADS):
        kvh = h // group
        qh = q_ref[:, h * HEAD_DIM:(h + 1) * HEAD_DIM].astype(BF16)
        cq = cq_ref[0, :, h:h + 1]

        def body(c, carry, qh=qh, cq=cq, kvh=kvh, h=h):
            start = pl.multiple_of(c * TK, TK)
            k = kv_ref[0, pl.ds(start, TK), kvh * HEAD_DIM:(kvh + 1) * HEAD_DIM].astype(BF16)
            v = kv_ref[0, pl.ds(start, TK), 128 + kvh * HEAD_DIM:128 + (kvh + 1) * HEAD_DIM].astype(BF16)
            s = _nt_dot(qh, k) * ATT_SCALE
            s = s + (cq - ck_ref[0, c, h:h + 1, :])
            cols = start + lax.broadcasted_iota(I32, (1, TK), 1)
            s = jnp.where(cols <= rows, s, NEG)
            return _online_update(s, v, *carry)

        init = (jnp.full((TQ, 1), NEG, F32), jnp.zeros((TQ, 1), F32), jnp.zeros((TQ, HEAD_DIM), F32))
        m, l, acc = lax.fori_loop(0, i + 1, body, init)
        outs.append(acc / l)
    o_ref[...] = jnp.concatenate(outs, axis=1).astype(BF16)


def fox_prompt(z, foxkv, cum, cumt, b, t):
    nq = t // TQ
    return pl.pallas_call(
        _fox_prompt_body,
        grid=(b, nq),
        in_specs=[
            pl.BlockSpec((TQ, 512), lambda bi, i: (bi * nq + i, C_FOXQ // 512)),
            pl.BlockSpec((1, t, 256), lambda bi, i: (bi, 0, 0)),
            pl.BlockSpec((1, TQ, N_HEADS), lambda bi, i: (bi, i, 0)),
            pl.BlockSpec((1, t // TK, N_HEADS, TK), lambda bi, i: (bi, 0, 0, 0)),
        ],
        out_specs=pl.BlockSpec((TQ, 512), lambda bi, i: (bi * nq + i, 0)),
        out_shape=jax.ShapeDtypeStruct((b * t, 512), BF16),
        compiler_params=_cparams(2),
        name="fox_prompt",
    )(z, foxkv, cum, cumt)


def _mla_prompt_body(q_ref, kv_ref, o_ref):
    i = pl.program_id(1)
    rows = i * TQ + lax.broadcasted_iota(I32, (TQ, 1), 0)
    outs = []
    for h in range(N_HEADS):
        qh = q_ref[:, h * MLAQ_SLOT:h * MLAQ_SLOT + MLAQ_DIM]

        def body(c, carry, qh=qh):
            start = pl.multiple_of(c * TK, TK)
            kc = kv_ref[0, pl.ds(start, TK), :].astype(BF16)
            s = _nt_dot(qh, kc) * MLA_SCALE
            cols = start + lax.broadcasted_iota(I32, (1, TK), 1)
            s = jnp.where(cols <= rows, s, NEG)
            return _online_update(s, kc[:, :MLA_RANK], *carry)

        init = (jnp.full((TQ, 1), NEG, F32), jnp.zeros((TQ, 1), F32), jnp.zeros((TQ, MLA_RANK), F32))
        m, l, acc = lax.fori_loop(0, i + 1, body, init)
        outs.append(acc / l)
    o_ref[...] = jnp.concatenate(outs, axis=1).astype(BF16)


def mla_prompt(qc, mla_rows, b, t):
    nq = t // TQ
    return pl.pallas_call(
        _mla_prompt_body,
        grid=(b, nq),
        in_specs=[
            pl.BlockSpec((TQ, N_HEADS * MLAQ_SLOT), lambda bi, i: (bi * nq + i, 0)),
            pl.BlockSpec((1, t, MLAQ_DIM), lambda bi, i: (bi, 0, 0)),
        ],
        out_specs=pl.BlockSpec((TQ, N_HEADS * MLA_RANK), lambda bi, i: (bi * nq + i, 0)),
        out_shape=jax.ShapeDtypeStruct((b * t, N_HEADS * MLA_RANK), BF16),
        compiler_params=_cparams(2),
        name="mla_prompt",
    )(qc, mla_rows)


def _sortable(score):
    bits = lax.bitcast_convert_type(score + 0.0, I32)
    return bits ^ ((bits >> 31) & jnp.int32(0x7FFFFFFF))


def _ones_where(mask):
    return jnp.where(mask, 1.0, 0.0).astype(F32)


def _kth_largest_key(count_ge, shape):
    def step(it, ut):
        bit = jnp.left_shift(jnp.int32(1), 31 - it)
        cand_u = ut | bit
        cnt = count_ge(cand_u ^ jnp.int32(INT_MIN))
        return jnp.where(cnt >= IDX_TOPK, cand_u, ut)

    ut = lax.fori_loop(0, 32, step, jnp.zeros(shape, I32))
    return ut ^ jnp.int32(INT_MIN)


def _dsa_prompt_body(q_ref, iq_ref, sm_ref, kv_ref, bias_ref, o_ref, key_ref, sel_ref):
    i = pl.program_id(1)
    nch = i + 1
    rows = i * TQ + lax.broadcasted_iota(I32, (TQ, 1), 0)
    lane = lax.broadcasted_iota(I32, (1, TK), 1)
    iw = sm_ref[:, C_IDXW:C_IDXW + IDX_HEADS] * IDX_W_SCALE

    def score_chunk(c, _):
        start = pl.multiple_of(c * TK, TK)
        ik = kv_ref[0, pl.ds(start, TK), 128:192].astype(BF16)
        score = jnp.zeros((TQ, TK), F32)
        for hh in range(IDX_HEADS):
            d = _nt_dot(iq_ref[:, hh * 64:(hh + 1) * 64].astype(BF16), ik) * IDX_SCALE
            score = score + jnp.maximum(d, 0.0) * iw[:, hh:hh + 1]
        causal = (start + lane) <= rows
        key_ref[c] = jnp.where(causal, _sortable(score), jnp.int32(INT_MIN))
        return 0

    lax.fori_loop(0, nch, score_chunk, 0)

    def count_ge(cand):
        def f(c, acc):
            return acc + _ones_where(key_ref[c] >= cand)
        acc = lax.fori_loop(0, nch, f, jnp.zeros((TQ, TK), F32))
        return jnp.sum(acc, axis=-1, keepdims=True)

    thr = _kth_largest_key(count_ge, (TQ, 1))

    def count_gt_f(c, acc):
        return acc + _ones_where(key_ref[c] > thr)

    cnt_gt = jnp.sum(lax.fori_loop(0, nch, count_gt_f, jnp.zeros((TQ, TK), F32)), axis=-1, keepdims=True)
    need = IDX_TOPK - cnt_gt

    r2 = lax.broadcasted_iota(I32, (TK, 2 * TK), 0)
    c2 = lax.broadcasted_iota(I32, (TK, 2 * TK), 1)
    tri_ones = jnp.where((r2 < c2) | (c2 >= TK), 1.0, 0.0).astype(BF16)

    def sel_chunk(c, before):
        start = pl.multiple_of(c * TK, TK)
        key = key_ref[c]
        eq = key == thr
        pre = jnp.dot(jnp.where(eq, 1.0, 0.0).astype(BF16), tri_ones, preferred_element_type=F32)
        prefix = pre[:, :TK] + before
        causal = (start + lane) <= rows
        sel = ((key > thr) | (eq & (prefix < need))) & causal
        sel_ref[c] = jnp.where(sel, 0.0, NEG)
        return before + pre[:, TK:TK + 1]

    lax.fori_loop(0, nch, sel_chunk, jnp.zeros((TQ, 1), F32))

    outs = []
    for h in range(N_HEADS):
        qh = q_ref[:, h * HEAD_DIM:(h + 1) * HEAD_DIM].astype(BF16)

        def body(c, carry, qh=qh, h=h):
            start = pl.multiple_of(c * TK, TK)
            k = kv_ref[0, pl.ds(start, TK), 0:64].astype(BF16)
            v = kv_ref[0, pl.ds(start, TK), 64:128].astype(BF16)
            s = _nt_dot(qh, k) * ATT_SCALE + bias_ref[h, jnp.minimum(i - c, 2)]
            s = s + sel_ref[c]
            return _online_update(s, v, *carry)

        init = (jnp.full((TQ, 1), NEG, F32), jnp.zeros((TQ, 1), F32), jnp.zeros((TQ, HEAD_DIM), F32))
        m, l, acc = lax.fori_loop(0, nch, body, init)
        outs.append(acc / l)
    o_ref[...] = jnp.concatenate(outs, axis=1).astype(BF16)


def dsa_prompt(z, dsa_rows, bias_tab, b, t):
    nq = t // TQ
    return pl.pallas_call(
        _dsa_prompt_body,
        grid=(b, nq),
        in_specs=[
            pl.BlockSpec((TQ, 512), lambda bi, i: (bi * nq + i, C_DSAQ // 512)),
            pl.BlockSpec((TQ, 256), lambda bi, i: (bi * nq + i, C_IDXQ // 256)),
            pl.BlockSpec((TQ, Z_SMALL), lambda bi, i: (bi * nq + i, 0)),
            pl.BlockSpec((1, t, 192), lambda bi, i: (bi, 0, 0)),
            pl.BlockSpec((N_HEADS, 3, TQ, TK), lambda bi, i: (0, 0, 0, 0)),
        ],
        out_specs=pl.BlockSpec((TQ, 512), lambda bi, i: (bi * nq + i, 0)),
        out_shape=jax.ShapeDtypeStruct((b * t, 512), BF16),
        scratch_shapes=[pltpu.VMEM((t // TK, TQ, TK), I32), pltpu.VMEM((t // TK, TQ, TK), F32)],
        compiler_params=_cparams(2),
        name="dsa_prompt",
    )(z, z, z, dsa_rows, bias_tab)


CH = PG * PAGE


def _page_specs(width, n_pages):
    def spec(j):
        return pl.BlockSpec((1, PAGE, width), lambda b, g, pt, j=j: (pt[b * n_pages + g * PG + j], 0, 0))
    return [spec(j) for j in range(PG)]


def _page_specs_rev(width, n_pages):
    ng = n_pages // PG

    def spec(j):
        return pl.BlockSpec((1, PAGE, width), lambda b, g, pt, j=j: (pt[b * n_pages + (ng - 1 - g) * PG + j], 0, 0))
    return [spec(j) for j in range(PG)]


def _finish_with_new_row(m, l, acc, s_new, v_new):
    m_tot = jnp.maximum(m, s_new)
    a = jnp.exp(m - m_tot)
    p_new = jnp.exp(s_new - m_tot)
    l_tot = a * l + p_new
    acc_tot = a * acc + p_new.astype(BF16).astype(F32) * v_new.astype(BF16).astype(F32)
    return acc_tot / l_tot


def _fox_sample_body(pt_ref, *refs):
    kv_refs = refs[:PG]
    lf_refs = refs[PG:2 * PG]
    q_ref, kvn_ref, lfn_ref, o_ref, m_ref, l_ref, acc_ref, suf_ref = refs[2 * PG:]
    g = pl.program_id(1)

    @pl.when(g == 0)
    def _():
        m_ref[...] = jnp.full(m_ref.shape, NEG, F32)
        l_ref[...] = jnp.zeros(l_ref.shape, F32)
        acc_ref[...] = jnp.zeros(acc_ref.shape, F32)
        suf_ref[...] = lfn_ref[0]

    q = q_ref[0].astype(BF16)
    r = lax.broadcasted_iota(I32, (PAGE, PAGE), 0)
    c = lax.broadcasted_iota(I32, (PAGE, PAGE), 1)
    after = (c > r).astype(F32)
    suf = suf_ref[...]
    decs = [None] * PG
    for j in reversed(range(PG)):
        lf = lf_refs[j][0]
        dec = jnp.dot(after, lf, precision=HIGHEST, preferred_element_type=F32) + suf
        suf = suf + jnp.sum(lf, axis=0, keepdims=True)
        decs[j] = dec.T
    suf_ref[...] = suf
    decay = jnp.concatenate(decs, axis=1)
    kv = jnp.concatenate([kv_refs[j][0] for j in range(PG)], axis=0).astype(BF16)
    s = _nt_dot(q, kv[:, :128]) * ATT_SCALE + decay
    m, l, acc = _online_update(s, kv[:, 128:], m_ref[...], l_ref[...], acc_ref[...])
    m_ref[...] = m
    l_ref[...] = l
    acc_ref[...] = acc

    @pl.when(g == pl.num_programs(1) - 1)
    def _():
        kvn = kvn_ref[0]
        qb = q_ref[0].astype(BF16).astype(F32)
        kn = kvn[:, :128].astype(BF16).astype(F32)
        s_new = jnp.sum(qb * kn, axis=-1, keepdims=True) * ATT_SCALE
        o_ref[0] = _finish_with_new_row(m, l, acc, s_new, kvn[:, 128:])


def fox_sample(pt, cache_kv, cache_lf, q8, kv_new, lf_new, n_pages):
    nb = q8.shape[0]
    ng = n_pages // PG
    fixed = lambda b, g, pt: (b, 0, 0)
    grid_spec = pltpu.PrefetchScalarGridSpec(
        num_scalar_prefetch=1,
        grid=(nb, ng),
        in_specs=_page_specs_rev(256, n_pages) + _page_specs_rev(N_HEADS, n_pages) + [
            pl.BlockSpec((1, N_HEADS, 128), fixed),
            pl.BlockSpec((1, 1, 256), fixed),
            pl.BlockSpec((1, 1, N_HEADS), fixed),
        ],
        out_specs=pl.BlockSpec((1, N_HEADS, 128), fixed),
        scratch_shapes=[pltpu.VMEM((N_HEADS, 1), F32), pltpu.VMEM((N_HEADS, 1), F32),
                        pltpu.VMEM((N_HEADS, 128), F32), pltpu.VMEM((1, N_HEADS), F32)],
    )
    return pl.pallas_call(
        _fox_sample_body,
        grid_spec=grid_spec,
        out_shape=jax.ShapeDtypeStruct((nb, N_HEADS, 128), F32),
        compiler_params=_cparams(2),
        name="fox_sample",
    )(pt, *([cache_kv] * PG), *([cache_lf] * PG), q8, kv_new, lf_new)


def _mla_sample_body(pt_ref, *refs):
    kv_refs = refs[:PG]
    q_ref, kvn_ref, o_ref, m_ref, l_ref, acc_ref = refs[PG:]
    g = pl.program_id(1)

    @pl.when(g == 0)
    def _():
        m_ref[...] = jnp.full(m_ref.shape, NEG, F32)
        l_ref[...] = jnp.zeros(l_ref.shape, F32)
        acc_ref[...] = jnp.zeros(acc_ref.shape, F32)

    q = q_ref[0]
    kc = jnp.concatenate([kv_refs[j][0] for j in range(PG)], axis=0).astype(BF16)
    s = _nt_dot(q, kc) * MLA_SCALE
    m, l, acc = _online_update(s, kc[:, :MLA_RANK], m_ref[...], l_ref[...], acc_ref[...])
    m_ref[...] = m
    l_ref[...] = l
    acc_ref[...] = acc

    @pl.when(g == pl.num_programs(1) - 1)
    def _():
        kvn = kvn_ref[0]
        kn = kvn.astype(BF16).astype(F32)
        s_new = jnp.sum(q.astype(F32) * kn, axis=-1, keepdims=True) * MLA_SCALE
        o_ref[0] = _finish_with_new_row(m, l, acc, s_new, kvn[:, :MLA_RANK])


def mla_sample(pt, cache_mla, qc, row_new, n_pages):
    nb = qc.shape[0]
    ng = n_pages // PG
    fixed = lambda b, g, pt: (b, 0, 0)
    grid_spec = pltpu.PrefetchScalarGridSpec(
        num_scalar_prefetch=1,
        grid=(nb, ng),
        in_specs=_page_specs(MLAQ_DIM, n_pages) + [
            pl.BlockSpec((1, N_HEADS, MLAQ_DIM), fixed),
            pl.BlockSpec((1, 1, MLAQ_DIM), fixed),
        ],
        out_specs=pl.BlockSpec((1, N_HEADS, MLA_RANK), fixed),
        scratch_shapes=[pltpu.VMEM((N_HEADS, 1), F32), pltpu.VMEM((N_HEADS, 1), F32),
                        pltpu.VMEM((N_HEADS, MLA_RANK), F32)],
    )
    return pl.pallas_call(
        _mla_sample_body,
        grid_spec=grid_spec,
        out_shape=jax.ShapeDtypeStruct((nb, N_HEADS, MLA_RANK), F32),
        compiler_params=_cparams(2),
        name="mla_sample",
    )(pt, *([cache_mla] * PG), qc, row_new)


def _dsa_sample_body(pt_ref, *refs):
    kv_refs = refs[:PG]
    q_ref, iq_ref, iw_ref, rown_ref, bias_ref, biasn_ref, o_ref, sc_ref, s_ref, v_ref = refs[PG:]
    g = pl.program_id(1)
    ng = pl.num_programs(1)
    kv = jnp.concatenate([kv_refs[j][0] for j in range(PG)], axis=0).astype(BF16)
    iq = iq_ref[0].astype(BF16)
    iw = iw_ref[0] * IDX_W_SCALE
    d = _nt_dot(iq, kv[:, 128:192]) * IDX_SCALE
    score = jnp.maximum(d[0:1], 0.0) * iw[0:1]
    for hh in range(1, IDX_HEADS):
        score = score + jnp.maximum(d[hh:hh + 1], 0.0) * iw[hh:hh + 1]
    sc_ref[pl.ds(g, 1), :] = score
    q = q_ref[0].astype(BF16)
    s_ref[g] = _nt_dot(q, kv[:, 0:64]) * ATT_SCALE + bias_ref[g]
    start = pl.multiple_of(g * CH, CH)
    v_ref[pl.ds(start, CH), :] = kv[:, 64:128]

    @pl.when(g == ng - 1)
    def _():
        n_groups = sc_ref.shape[0]
        rown = rown_ref[0]
        ikn = rown[:, 128:192].astype(BF16).astype(F32)
        dn = jnp.sum(iq.astype(F32) * ikn, axis=-1, keepdims=True) * IDX_SCALE
        score_n = jnp.maximum(dn[0:1], 0.0) * iw[0:1]
        for hh in range(1, IDX_HEADS):
            score_n = score_n + jnp.maximum(dn[hh:hh + 1], 0.0) * iw[hh:hh + 1]
        key = _sortable(sc_ref[...])
        key_n = _sortable(score_n)

        def count_ge(cand):
            cnt = jnp.sum(jnp.sum(_ones_where(key >= cand), axis=-1, keepdims=True), axis=0, keepdims=True)
            return cnt + _ones_where(key_n >= cand)

        thr = _kth_largest_key(count_ge, (1, 1))
        gt = key > thr
        cnt_gt = (jnp.sum(jnp.sum(_ones_where(gt), axis=-1, keepdims=True), axis=0, keepdims=True)
                  + _ones_where(key_n > thr))
        need = IDX_TOPK - cnt_gt
        eq = key == thr
        eqf = _ones_where(eq)
        r2 = lax.broadcasted_iota(I32, (CH, CH), 0)
        c2 = lax.broadcasted_iota(I32, (CH, CH), 1)
        upper = jnp.where(r2 < c2, 1.0, 0.0).astype(BF16)
        within = jnp.dot(eqf.astype(BF16), upper, preferred_element_type=F32)
        row_tot = jnp.sum(eqf, axis=-1, keepdims=True)
        total_eq = jnp.zeros((1, 1), F32)
        above = []
        for gg in range(n_groups):
            above.append(total_eq)
            total_eq = total_eq + row_tot[gg:gg + 1]
        prefix = within + jnp.concatenate(above, axis=0)
        sel = gt | (eq & (prefix < need))
        sel_n = (key_n > thr) | ((key_n == thr) & (total_eq < need))
        selb = jnp.where(sel, 0.0, NEG)
        selb_n = jnp.where(sel_n, 0.0, NEG)

        kn = rown[:, 0:64].astype(BF16).astype(F32)
        s_new = jnp.sum(q.astype(F32) * kn, axis=-1, keepdims=True) * ATT_SCALE + biasn_ref[...] + selb_n
        m = s_new
        rows = []
        for gg in range(n_groups):
            sg = s_ref[gg] + selb[gg:gg + 1, :]
            rows.append(sg)
            m = jnp.maximum(m, jnp.max(sg, axis=-1, keepdims=True))
        p_new = jnp.exp(s_new - m)
        l = p_new
        acc = p_new.astype(BF16).astype(F32) * rown[:, 64:128].astype(BF16).astype(F32)
        for gg in range(n_groups):
            p = jnp.exp(rows[gg] - m)
            l = l + jnp.sum(p, axis=-1, keepdims=True)
            acc = acc + jnp.dot(p.astype(BF16), v_ref[gg * CH:(gg + 1) * CH, :], preferred_element_type=F32)
        o_ref[0] = acc / l


def dsa_sample(pt, cache_dsa, q, iq, iw, row_new, bias_s, bias_n, n_pages):
    nb = q.shape[0]
    ng = n_pages // PG
    fixed = lambda b, g, pt: (b, 0, 0)
    grid_spec = pltpu.PrefetchScalarGridSpec(
        num_scalar_prefetch=1,
        grid=(nb, ng),
        in_specs=_page_specs(192, n_pages) + [
            pl.BlockSpec((1, N_HEADS, HEAD_DIM), fixed),
            pl.BlockSpec((1, IDX_HEADS, 64), fixed),
            pl.BlockSpec((1, IDX_HEADS, 1), fixed),
            pl.BlockSpec((1, 1, 192), fixed),
            pl.BlockSpec((ng, N_HEADS, CH), lambda b, g, pt: (0, 0, 0)),
            pl.BlockSpec((N_HEADS, 1), lambda b, g, pt: (0, 0)),
        ],
        out_specs=pl.BlockSpec((1, N_HEADS, HEAD_DIM), fixed),
        scratch_shapes=[pltpu.VMEM((ng, CH), F32), pltpu.VMEM((ng, N_HEADS, CH), F32),
                        pltpu.VMEM((ng * CH, HEAD_DIM), BF16)],
    )
    return pl.pallas_call(
        _dsa_sample_body,
        grid_spec=grid_spec,
        out_shape=jax.ShapeDtypeStruct((nb, N_HEADS, HEAD_DIM), F32),
        compiler_params=_cparams(2),
        name="dsa_sample",
    )(pt, *([cache_dsa] * PG), q, iq, iw, row_new, bias_s, bias_n)


def _t5_bucket(dist):
    max_exact = NUM_BUCKETS // 2
    d = jnp.maximum(dist, 0)
    log_ratio = jnp.log(jnp.maximum(d, max_exact).astype(F32) / max_exact) / math.log(MAX_DISTANCE / max_exact)
    large = jnp.minimum(max_exact + (log_ratio * (NUM_BUCKETS - max_exact)).astype(I32), NUM_BUCKETS - 1)
    return jnp.where(d < max_exact, d, large)


def _rope_tables(pos, reps):
    half = MLA_ROPE // 2
    inv = ROPE_BASE ** (-jnp.arange(half, dtype=F32) / half)
    ang = pos.astype(F32)[:, None] * inv[None, :]
    return jnp.tile(jnp.cos(ang), (1, reps)), jnp.tile(jnp.sin(ang), (1, reps))


def _prep_layer_weights(w_in, w_mla_uk, w_mla_uv, w_branch_fox, w_branch_dsa, w_branch_mla, w_out, w_up, w_down):
    depth, d, _ = w_in.shape
    off = {}
    o = 0
    for name, width in (('fox_q', 512), ('fox_k', 128), ('fox_v', 128), ('fox_f', 8), ('dsa_q', 512), ('dsa_k', 64),
                        ('dsa_v', 64), ('idx_q', 256), ('idx_k', 64), ('idx_w', 4), ('mla_q', 768), ('mla_ckv', 128),
                        ('mla_kr', 32), ('gates', 3072)):
        off[name] = (o, width)
        o += width

    def seg(name):
        a, w = off[name]
        return w_in[:, :, a:a + w]

    def pad(w):
        return jnp.zeros((depth, d, w), w_in.dtype)

    mq = seg('mla_q').reshape(depth, d, N_HEADS, MLA_NOPE + MLA_ROPE)
    half = MLA_ROPE // 2
    mla_q = jnp.concatenate([
        mq[..., :MLA_NOPE].reshape(depth, d, N_HEADS * MLA_NOPE),
        mq[..., MLA_NOPE:MLA_NOPE + half].reshape(depth, d, N_HEADS * half),
        mq[..., MLA_NOPE + half:].reshape(depth, d, N_HEADS * half)], axis=-1)
    cols = [seg('fox_k'), seg('fox_v'), seg('dsa_k'), seg('dsa_v'), seg('idx_k'), pad(64),
            seg('mla_ckv'), seg('mla_kr'), seg('fox_f'), seg('idx_w'), pad(Z_SMALL - 684),
            mla_q, seg('fox_q'), seg('dsa_q'), seg('idx_q'), pad(C_GATES - C_IDXQ - 256), seg('gates')]
    wz = jnp.concatenate(cols, axis=-1).astype(BF16)
    assert wz.shape[-1] == Z_WIDTH
    return dict(
        wz=wz,
        wuk=jnp.transpose(w_mla_uk, (0, 2, 3, 1)).astype(BF16),
        wuv=jnp.transpose(w_mla_uv, (0, 2, 1, 3)).astype(BF16),
        wbf=w_branch_fox.astype(BF16), wbd=w_branch_dsa.astype(BF16), wbm=w_branch_mla.astype(BF16),
        wo=w_out.astype(BF16), wup=w_up.astype(BF16), wdn=w_down.astype(BF16),
    )


def _prompt_bias_tiles(rel_bias):
    r = jnp.arange(TQ)[:, None]
    c = jnp.arange(TK)[None, :]
    d = jnp.stack([jnp.maximum(r - c, 0), r - c + TK, jnp.full((TQ, TK), 2 * TK, I32)], axis=0)
    return jnp.transpose(rel_bias[_t5_bucket(d)], (3, 0, 1, 2))


def _trunk_prompt(x, lw, g_attn, b_forget, g_mla_kv, g_mlp, rel_bias, g_final):
    b, t, d = x.shape
    n = b * t
    x = x.reshape(n, d)
    pos = jnp.arange(t, dtype=I32)
    cos32, sin_h = _rope_tables(pos, 1)
    cos32 = jnp.concatenate([cos32, cos32], axis=1)
    sin32 = jnp.concatenate([-sin_h, sin_h], axis=1)
    cos128, sin128 = _rope_tables(pos, N_HEADS)
    bias_tab = _prompt_bias_tiles(rel_bias)
    rows = ([], [], [], [])
    tm = 1024
    for l in range(DEPTH):
        z = rms_matmul(x, g_attn[l][None], lw['wz'][l], tm, 1024)
        foxkv, logf, dsa, mla = post_rows(z, g_mla_kv[l][None], b_forget[l][None], cos32, sin32, tm)
        cum, cumt = fox_cumsum(logf.reshape(b, t, N_HEADS))
        o_fox = fox_prompt(z, foxkv.reshape(b, t, 256), cum, cumt, b, t)
        o_dsa = dsa_prompt(z, dsa.reshape(b, t, 192), bias_tab, b, t)
        qc = mla_q_prep(z, cos128, sin128, lw['wuk'][l], tm)
        o_lat = mla_prompt(qc, mla.reshape(b, t, MLAQ_DIM), b, t)
        o_mla = mla_uv(o_lat, lw['wuv'][l], tm)
        x = merge_out(o_fox, o_dsa, o_mla, z, lw['wbf'][l], lw['wbd'][l], lw['wbm'][l], lw['wo'][l], x, 512)
        x = mlp_residual(x, g_mlp[l][None], lw['wup'][l], lw['wdn'][l], tm, 1024)
        rows[0].append(foxkv.reshape(b, t, 2, FOX_KV_HEADS, HEAD_DIM))
        rows[1].append(logf.reshape(b, t, N_HEADS))
        rows[2].append(dsa.reshape(b, t, 3, HEAD_DIM))
        rows[3].append(mla.reshape(b, t, MLAQ_DIM))
    y = rms_only(x, g_final[None], tm).reshape(b, t, d)
    return y, tuple(jnp.stack(r, axis=0) for r in rows)


def _trunk_sample(x, caches, page_table, lw, g_attn, b_forget, g_mla_kv, g_mlp, rel_bias, g_final):
    cache_fox_kv, cache_fox_logf, cache_dsa_kv, cache_mla = caches
    nb, t, d = x.shape
    n_pages = page_table.shape[1]
    past = n_pages * PAGE
    ng = n_pages // PG
    n_pool = cache_fox_kv.shape[1]
    x = x.reshape(nb, d)
    pos = jnp.full((nb,), past, I32)
    cos32, sin_h = _rope_tables(pos, 1)
    cos32 = jnp.concatenate([cos32, cos32], axis=1)
    sin32 = jnp.concatenate([-sin_h, sin_h], axis=1)
    cos128, sin128 = _rope_tables(pos, N_HEADS)
    by_dist = rel_bias[_t5_bucket(jnp.arange(past + 1))]
    bias_s = jnp.transpose(by_dist[past - jnp.arange(past)].reshape(ng, CH, N_HEADS), (0, 2, 1))
    bias_n = by_dist[0][:, None]
    c_fox = cache_fox_kv.reshape(DEPTH * n_pool, PAGE, 256)
    c_lf = cache_fox_logf.reshape(DEPTH * n_pool, PAGE, N_HEADS)
    c_dsa = cache_dsa_kv.reshape(DEPTH * n_pool, PAGE, 192)
    c_mla = cache_mla.reshape(DEPTH * n_pool, PAGE, MLAQ_DIM)
    group = N_HEADS // FOX_KV_HEADS
    half_mask = (jnp.arange(N_HEADS)[:, None] // group == jnp.arange(128)[None, :] // HEAD_DIM)
    rows = ([], [], [], [])
    tm = nb
    for l in range(DEPTH):
        pt = (page_table + l * n_pool).reshape(-1).astype(I32)
        z = rms_matmul(x, g_attn[l][None], lw['wz'][l], tm, 1024)
        foxkv, logf, dsa, mla = post_rows(z, g_mla_kv[l][None], b_forget[l][None], cos32, sin32, tm)
        fq = z[:, C_FOXQ:C_FOXQ + 512].reshape(nb, N_HEADS, HEAD_DIM)
        q8 = jnp.where(half_mask[None], jnp.concatenate([fq, fq], axis=-1), 0.0)
        o8 = fox_sample(pt, c_fox, c_lf, q8, foxkv[:, None, :], logf[:, None, :], n_pages)
        o8 = o8.reshape(nb, N_HEADS, FOX_KV_HEADS, HEAD_DIM)
        o_fox = jnp.concatenate([o8[:, :group, 0], o8[:, group:, 1]], axis=1).reshape(nb, 512).astype(BF16)
        dq = z[:, C_DSAQ:C_DSAQ + 512].reshape(nb, N_HEADS, HEAD_DIM)
        iq = z[:, C_IDXQ:C_IDXQ + 256].reshape(nb, IDX_HEADS, 64)
        iw = z[:, C_IDXW:C_IDXW + IDX_HEADS].reshape(nb, IDX_HEADS, 1)
        o_dsa = dsa_sample(pt, c_dsa, dq, iq, iw, dsa[:, None, :], bias_s, bias_n, n_pages)
        o_dsa = o_dsa.reshape(nb, 512).astype(BF16)
        qc = mla_q_prep(z, cos128, sin128, lw['wuk'][l], tm)
        qc = qc.reshape(nb, N_HEADS, MLAQ_SLOT)[:, :, :MLAQ_DIM]
        o_lat = mla_sample(pt, c_mla, qc, mla[:, None, :], n_pages)
        o_mla = mla_uv(o_lat.reshape(nb, N_HEADS * MLA_RANK).astype(BF16), lw['wuv'][l], tm)
        x = merge_out(o_fox, o_dsa, o_mla, z, lw['wbf'][l], lw['wbd'][l], lw['wbm'][l], lw['wo'][l], x, tm)
        x = mlp_residual(x, g_mlp[l][None], lw['wup'][l], lw['wdn'][l], tm, 1024)
        rows[0].append(foxkv.reshape(nb, 1, 2, FOX_KV_HEADS, HEAD_DIM))
        rows[1].append(logf.reshape(nb, 1, N_HEADS))
        rows[2].append(dsa.reshape(nb, 1, 3, HEAD_DIM))
        rows[3].append(mla.reshape(nb, 1, MLAQ_DIM))
    y = rms_only(x, g_final[None], tm).reshape(nb, 1, d)
    return y, tuple(jnp.stack(r, axis=0) for r in rows)


def kernel(x_prompt, x_sample, cache_fox_kv, cache_fox_logf, cache_dsa_kv, cache_mla, page_table, g_attn, w_in,
           b_forget, g_mla_kv, w_mla_uk, w_mla_uv, w_branch_fox, w_branch_dsa, w_branch_mla, w_out, g_mlp, w_up,
           w_down, rel_bias, g_final):
    lw = _prep_layer_weights(w_in, w_mla_uk, w_mla_uv, w_branch_fox, w_branch_dsa, w_branch_mla, w_out, w_up, w_down)
    y_p, (fox_kv_p, fox_logf_p, dsa_kv_p, mla_p) = _trunk_prompt(
        x_prompt, lw, g_attn, b_forget, g_mla_kv, g_mlp, rel_bias, g_final)
    y_s, (fox_kv_s, fox_logf_s, dsa_kv_s, mla_s) = _trunk_sample(
        x_sample, (cache_fox_kv, cache_fox_logf, cache_dsa_kv, cache_mla), page_table, lw, g_attn, b_forget,
        g_mla_kv, g_mlp, rel_bias, g_final)
    return (y_p, y_s, fox_kv_p, fox_logf_p, dsa_kv_p, mla_p, fox_kv_s, fox_logf_s, dsa_kv_s, mla_s)
```

```python
import functools
import math

import jax
import jax.numpy as jnp
from jax import lax
from jax.experimental import pallas as pl
from jax.experimental.pallas import tpu as pltpu

F32 = jnp.float32
BF16 = jnp.bfloat16
I32 = jnp.int32
HIGHEST = lax.Precision.HIGHEST

D_MODEL = 1024
DEPTH = 4
PAGE = 128
HEAD_DIM = 64
N_HEADS = 8
FOX_KV_HEADS = 2
IDX_HEADS = 4
IDX_TOPK = 256
MLA_NOPE = 64
MLA_ROPE = 32
MLA_RANK = 128
MLA_SCALE = (MLA_NOPE + MLA_ROPE) ** -0.5
ROPE_BASE = 10000.0
NUM_BUCKETS = 32
MAX_DISTANCE = 128
D_FF = 4 * D_MODEL
EPS = 1e-6
ATT_SCALE = HEAD_DIM ** -0.5
IDX_SCALE = 64 ** -0.5
IDX_W_SCALE = IDX_HEADS ** -0.5

NEG = -1e30
INT_MIN = -(2 ** 31)

Z_WIDTH = 6144
Z_SMALL = 768
C_FOXKV, C_DSA, C_CKV, C_KR, C_FOXF, C_IDXW = 0, 256, 512, 640, 672, 680
C_MLAQ = 768
C_FOXQ = 1536
C_DSAQ = 2048
C_IDXQ = 2560
C_GATES = 3072

VMEM_LIMIT = 52 * 1024 * 1024

TQ = 256
TK = 256
PG = 8


def _cparams(n_axes):
    return pltpu.CompilerParams(dimension_semantics=("arbitrary",) * n_axes, vmem_limit_bytes=VMEM_LIMIT)


def _rms_mm_body(x_ref, g_ref, w_ref, o_ref, xn_ref):
    @pl.when(pl.program_id(1) == 0)
    def _():
        x = x_ref[...]
        y = x * lax.rsqrt(jnp.mean(x * x, axis=-1, keepdims=True) + EPS)
        xn_ref[...] = (y * g_ref[...]).astype(BF16)

    o_ref[...] = jnp.dot(xn_ref[...], w_ref[...], preferred_element_type=F32)


def rms_matmul(x, g, w, tm, tn):
    n, d = x.shape
    e = w.shape[1]
    return pl.pallas_call(
        _rms_mm_body,
        grid=(n // tm, e // tn),
        in_specs=[
            pl.BlockSpec((tm, d), lambda i, j: (i, 0)),
            pl.BlockSpec((1, d), lambda i, j: (0, 0)),
            pl.BlockSpec((d, tn), lambda i, j: (0, j)),
        ],
        out_specs=pl.BlockSpec((tm, tn), lambda i, j: (i, j)),
        out_shape=jax.ShapeDtypeStruct((n, e), F32),
        scratch_shapes=[pltpu.VMEM((tm, d), BF16)],
        compiler_params=_cparams(2),
        name="rms_matmul",
    )(x, g, w)


def _post_body(z_ref, g_ref, b_ref, cos_ref, sin_ref, fox_ref, logf_ref, dsa_ref, mla_ref):
    fox_ref[...] = z_ref[:, C_FOXKV:C_FOXKV + 256]
    dsa_ref[...] = z_ref[:, C_DSA:C_DSA + 192]
    ckv = z_ref[:, C_CKV:C_CKV + MLA_RANK]
    ckvn = ckv * lax.rsqrt(jnp.mean(ckv * ckv, axis=-1, keepdims=True) + EPS) * g_ref[...]
    kr = z_ref[:, C_KR:C_KR + MLA_ROPE]
    half = MLA_ROPE // 2
    rot = jnp.concatenate([kr[:, half:], kr[:, :half]], axis=1)
    krr = kr * cos_ref[...] + rot * sin_ref[...]
    mla_ref[...] = jnp.concatenate([ckvn, krr], axis=1)
    logf_ref[...] = jax.nn.log_sigmoid(z_ref[:, C_FOXF:C_FOXF + N_HEADS] + b_ref[...])


def post_rows(z, g_mla, b_forget, cos32, sin32, tm):
    n = z.shape[0]
    nt = cos32.shape[0] // tm
    row = lambda i: (i, 0)
    tab = lambda i: (i % nt, 0)
    const = lambda i: (0, 0)
    return pl.pallas_call(
        _post_body,
        grid=(n // tm,),
        in_specs=[
            pl.BlockSpec((tm, Z_SMALL), row),
            pl.BlockSpec((1, MLA_RANK), const),
            pl.BlockSpec((1, N_HEADS), const),
            pl.BlockSpec((tm, MLA_ROPE), tab),
            pl.BlockSpec((tm, MLA_ROPE), tab),
        ],
        out_specs=[
            pl.BlockSpec((tm, 256), row),
            pl.BlockSpec((tm, N_HEADS), row),
            pl.BlockSpec((tm, 192), row),
            pl.BlockSpec((tm, MLA_RANK + MLA_ROPE), row),
        ],
        out_shape=[
            jax.ShapeDtypeStruct((n, 256), F32),
            jax.ShapeDtypeStruct((n, N_HEADS), F32),
            jax.ShapeDtypeStruct((n, 192), F32),
            jax.ShapeDtypeStruct((n, MLA_RANK + MLA_ROPE), F32),
        ],
        compiler_params=_cparams(1),
        name="post_rows",
    )(z, g_mla, b_forget, cos32, sin32)


MLAQ_SLOT = 256
MLAQ_DIM = MLA_RANK + MLA_ROPE


def _mla_q_body(z_ref, cos_ref, sin_ref, wuk_ref, o_ref):
    r1 = z_ref[:, 512:640]
    r2 = z_ref[:, 640:768]
    c = cos_ref[...]
    s = sin_ref[...]
    o1 = r1 * c - r2 * s
    o2 = r2 * c + r1 * s
    half = MLA_ROPE // 2
    tm = o1.shape[0]
    for h in range(N_HEADS):
        nope = z_ref[:, h * MLA_NOPE:(h + 1) * MLA_NOPE].astype(BF16)
        q_lat = jnp.dot(nope, wuk_ref[h], preferred_element_type=F32)
        base = h * MLAQ_SLOT
        o_ref[:, base:base + MLA_RANK] = q_lat.astype(BF16)
        q_rope = jnp.concatenate([o1[:, h * half:(h + 1) * half], o2[:, h * half:(h + 1) * half]], axis=1)
        o_ref[:, base + MLA_RANK:base + MLAQ_DIM] = q_rope.astype(BF16)
        o_ref[:, base + MLAQ_DIM:base + MLAQ_SLOT] = jnp.zeros((tm, MLAQ_SLOT - MLAQ_DIM), BF16)


def mla_q_prep(z, cos128, sin128, wuk, tm):
    n = z.shape[0]
    nt = cos128.shape[0] // tm
    tab = lambda i: (i % nt, 0)
    return pl.pallas_call(
        _mla_q_body,
        grid=(n // tm,),
        in_specs=[
            pl.BlockSpec((tm, 768), lambda i: (i, C_MLAQ // 768)),
            pl.BlockSpec((tm, 128), tab),
            pl.BlockSpec((tm, 128), tab),
            pl.BlockSpec((N_HEADS, MLA_NOPE, MLA_RANK), lambda i: (0, 0, 0)),
        ],
        out_specs=pl.BlockSpec((tm, N_HEADS * MLAQ_SLOT), lambda i: (i, 0)),
        out_shape=jax.ShapeDtypeStruct((n, N_HEADS * MLAQ_SLOT), BF16),
        compiler_params=_cparams(1),
        name="mla_q_prep",
    )(z, cos128, sin128, wuk)


def _mla_uv_body(o_ref, wuv_ref, out_ref):
    outs = []
    for h in range(N_HEADS):
        outs.append(jnp.dot(o_ref[:, h * MLA_RANK:(h + 1) * MLA_RANK], wuv_ref[h], preferred_element_type=F32))
    out_ref[...] = jnp.concatenate(outs, axis=1).astype(BF16)


def mla_uv(o_lat, wuv, tm):
    n = o_lat.shape[0]
    return pl.pallas_call(
        _mla_uv_body,
        grid=(n // tm,),
        in_specs=[
            pl.BlockSpec((tm, N_HEADS * MLA_RANK), lambda i: (i, 0)),
            pl.BlockSpec((N_HEADS, MLA_RANK, HEAD_DIM), lambda i: (0, 0, 0)),
        ],
        out_specs=pl.BlockSpec((tm, N_HEADS * HEAD_DIM), lambda i: (i, 0)),
        out_shape=jax.ShapeDtypeStruct((n, N_HEADS * HEAD_DIM), BF16),
        compiler_params=_cparams(1),
        name="mla_uv",
    )(o_lat, wuv)


def _merge_body(of_ref, od_ref, om_ref, g0_ref, g1_ref, g2_ref, wf_ref, wd_ref, wm_ref, wo_ref, x_ref, out_ref):
    merged = (jax.nn.sigmoid(g0_ref[...]) * jnp.dot(of_ref[...], wf_ref[...], preferred_element_type=F32)
              + jax.nn.sigmoid(g1_ref[...]) * jnp.dot(od_ref[...], wd_ref[...], preferred_element_type=F32)
              + jax.nn.sigmoid(g2_ref[...]) * jnp.dot(om_ref[...], wm_ref[...], preferred_element_type=F32))
    out_ref[...] = x_ref[...] + jnp.dot(merged.astype(BF16), wo_ref[...], preferred_element_type=F32)


def merge_out(o_fox, o_dsa, o_mla, z, wf, wd, wm, wo, x, tm):
    n = x.shape[0]
    row = lambda i: (i, 0)
    const = lambda i: (0, 0)
    gb = C_GATES // D_MODEL
    return pl.pallas_call(
        _merge_body,
        grid=(n // tm,),
        in_specs=[
            pl.BlockSpec((tm, 512), row),
            pl.BlockSpec((tm, 512), row),
            pl.BlockSpec((tm, 512), row),
            pl.BlockSpec((tm, D_MODEL), lambda i: (i, gb)),
            pl.BlockSpec((tm, D_MODEL), lambda i: (i, gb + 1)),
            pl.BlockSpec((tm, D_MODEL), lambda i: (i, gb + 2)),
            pl.BlockSpec((512, D_MODEL), const),
            pl.BlockSpec((512, D_MODEL), const),
            pl.BlockSpec((512, D_MODEL), const),
            pl.BlockSpec((D_MODEL, D_MODEL), const),
            pl.BlockSpec((tm, D_MODEL), row),
        ],
        out_specs=pl.BlockSpec((tm, D_MODEL), row),
        out_shape=jax.ShapeDtypeStruct((n, D_MODEL), F32),
        compiler_params=_cparams(1),
        name="merge_out",
    )(o_fox, o_dsa, o_mla, z, z, z, wf, wd, wm, wo, x)


def _mlp_body(x_ref, g_ref, wu_ref, wd_ref, o_ref, xn_ref, acc_ref):
    f = pl.program_id(1)

    @pl.when(f == 0)
    def _():
        x = x_ref[...]
        y = x * lax.rsqrt(jnp.mean(x * x, axis=-1, keepdims=True) + EPS)
        xn_ref[...] = (y * g_ref[...]).astype(BF16)
        acc_ref[...] = x

    h = jnp.maximum(jnp.dot(xn_ref[...], wu_ref[...], preferred_element_type=F32), 0.0)
    acc_ref[...] += jnp.dot((h * h).astype(BF16), wd_ref[...], preferred_element_type=F32)

    @pl.when(f == pl.num_programs(1) - 1)
    def _():
        o_ref[...] = acc_ref[...]


def mlp_residual(x, g, w_up, w_down, tm, tf):
    n, d = x.shape
    dff = w_up.shape[1]
    return pl.pallas_call(
        _mlp_body,
        grid=(n // tm, dff // tf),
        in_specs=[
            pl.BlockSpec((tm, d), lambda i, f: (i, 0)),
            pl.BlockSpec((1, d), lambda i, f: (0, 0)),
            pl.BlockSpec((d, tf), lambda i, f: (0, f)),
            pl.BlockSpec((tf, d), lambda i, f: (f, 0)),
        ],
        out_specs=pl.BlockSpec((tm, d), lambda i, f: (i, 0)),
        out_shape=jax.ShapeDtypeStruct((n, d), F32),
        scratch_shapes=[pltpu.VMEM((tm, d), BF16), pltpu.VMEM((tm, d), F32)],
        compiler_params=_cparams(2),
        name="mlp_residual",
    )(x, g, w_up, w_down)


def _rms_body(x_ref, g_ref, o_ref):
    x = x_ref[...]
    y = x * lax.rsqrt(jnp.mean(x * x, axis=-1, keepdims=True) + EPS)
    o_ref[...] = y * g_ref[...]


def rms_only(x, g, tm):
    n, d = x.shape
    return pl.pallas_call(
        _rms_body,
        grid=(n // tm,),
        in_specs=[pl.BlockSpec((tm, d), lambda i: (i, 0)), pl.BlockSpec((1, d), lambda i: (0, 0))],
        out_specs=pl.BlockSpec((tm, d), lambda i: (i, 0)),
        out_shape=jax.ShapeDtypeStruct((n, d), F32),
        compiler_params=_cparams(1),
        name="rms_only",
    )(x, g)


def _tri_incl(n):
    r = lax.broadcasted_iota(I32, (n, n), 0)
    c = lax.broadcasted_iota(I32, (n, n), 1)
    return (c <= r).astype(F32)


def _cumsum_body(lf_ref, cum_ref, cumt_ref):
    t = lf_ref.shape[1]
    tri = _tri_incl(TK)
    carry = jnp.zeros((1, N_HEADS), F32)
    for c in range(t // TK):
        lf = lf_ref[0, c * TK:(c + 1) * TK, :]
        cum = jnp.dot(tri, lf, precision=HIGHEST, preferred_element_type=F32) + carry
        carry = cum[TK - 1:TK, :]
        cum_ref[0, c * TK:(c + 1) * TK, :] = cum
        cumt_ref[0, c] = cum.T


def fox_cumsum(logf):
    b, t, h = logf.shape
    return pl.pallas_call(
        _cumsum_body,
        grid=(b,),
        in_specs=[pl.BlockSpec((1, t, h), lambda i: (i, 0, 0))],
        out_specs=[
            pl.BlockSpec((1, t, h), lambda i: (i, 0, 0)),
            pl.BlockSpec((1, t // TK, h, TK), lambda i: (i, 0, 0, 0)),
        ],
        out_shape=[
            jax.ShapeDtypeStruct((b, t, h), F32),
            jax.ShapeDtypeStruct((b, t // TK, h, TK), F32),
        ],
        compiler_params=_cparams(1),
        name="fox_cumsum",
    )(logf)


def _online_update(s, v, m, l, acc):
    m_new = jnp.maximum(m, jnp.max(s, axis=-1, keepdims=True))
    alpha = jnp.exp(m - m_new)
    p = jnp.exp(s - m_new)
    l_new = alpha * l + jnp.sum(p, axis=-1, keepdims=True)
    acc_new = alpha * acc + jnp.dot(p.astype(BF16), v, preferred_element_type=F32)
    return m_new, l_new, acc_new


def _nt_dot(a, b):
    return lax.dot_general(a, b, (((1,), (1,)), ((), ())), preferred_element_type=F32)


def _init_softmax_state(m_ref, l_ref, acc_ref):
    m_ref[...] = jnp.full(m_ref.shape, NEG, F32)
    l_ref[...] = jnp.zeros(l_ref.shape, F32)
    acc_ref[...] = jnp.zeros(acc_ref.shape, F32)


def _head_update(h, s, v, m_ref, l_ref, acc_ref):
    m, l, acc = _online_update(s, v, m_ref[h], l_ref[h], acc_ref[h])
    m_ref[h] = m
    l_ref[h] = l
    acc_ref[h] = acc


def _normalised_heads(l_ref, acc_ref):
    return jnp.concatenate([acc_ref[h] / l_ref[h] for h in range(N_HEADS)], axis=1).astype(BF16)


def _fox_prompt_body(q_ref, kv_ref, cq_ref, ck_ref, o_ref, qs_ref, m_ref, l_ref, acc_ref):
    i = pl.program_id(1)
    rows = i * TQ + lax.broadcasted_iota(I32, (TQ, 1), 0)
    lane = lax.broadcasted_iota(I32, (1, TK), 1)
    group = N_HEADS // FOX_KV_HEADS
    for h in range(N_HEADS):
        qs_ref[h] = q_ref[:, h * HEAD_DIM:(h + 1) * HEAD_DIM].astype(BF16)
    _init_softmax_state(m_ref, l_ref, acc_ref)

    def body(c, _):
        start = pl.multiple_of(c * TK, TK)
        kv = kv_ref[0, pl.ds(start, TK), :].astype(BF16)
        causal = (start + lane) <= rows
        for h in range(N_HEADS):
            kvh = h // group
            s = _nt_dot(qs_ref[h], kv[:, kvh * HEAD_DIM:(kvh + 1) * HEAD_DIM]) * ATT_SCALE
            s = s + (cq_ref[0, :, h:h + 1] - ck_ref[0, c, h:h + 1, :])
            s = jnp.where(causal, s, NEG)
            _head_update(h, s, kv[:, 128 + kvh * HEAD_DIM:128 + (kvh + 1) * HEAD_DIM], m_ref, l_ref, acc_ref)
        return 0

    lax.fori_loop(0, i + 1, body, 0)
    o_ref[...] = _normalised_heads(l_ref, acc_ref)


def _head_state_scratch(dk, dv):
    return [pltpu.VMEM((N_HEADS, TQ, dk), BF16), pltpu.VMEM((N_HEADS, TQ, 1), F32),
            pltpu.VMEM((N_HEADS, TQ, 1), F32), pltpu.VMEM((N_HEADS, TQ, dv), F32)]


def fox_prompt(z, foxkv, cum, cumt, b, t):
    nq = t // TQ
    return pl.pallas_call(
        _fox_prompt_body,
        grid=(b, nq),
        in_specs=[
            pl.BlockSpec((TQ, 512), lambda bi, i: (bi * nq + i, C_FOXQ // 512)),
            pl.BlockSpec((1, t, 256), lambda bi, i: (bi, 0, 0)),
            pl.BlockSpec((1, TQ, N_HEADS), lambda bi, i: (bi, i, 0)),
            pl.BlockSpec((1, t // TK, N_HEADS, TK), lambda bi, i: (bi, 0, 0, 0)),
        ],
        out_specs=pl.BlockSpec((TQ, 512), lambda bi, i: (bi * nq + i, 0)),
        out_shape=jax.ShapeDtypeStruct((b * t, 512), BF16),
        scratch_shapes=_head_state_scratch(HEAD_DIM, HEAD_DIM),
        compiler_params=_cparams(2),
        name="fox_prompt",
    )(z, foxkv, cum, cumt)


def _mla_prompt_body(q_ref, kv_ref, o_ref, m_ref, l_ref, acc_ref):
    i = pl.program_id(1)
    rows = i * TQ + lax.broadcasted_iota(I32, (TQ, 1), 0)
    lane = lax.broadcasted_iota(I32, (1, TK), 1)
    _init_softmax_state(m_ref, l_ref, acc_ref)

    def body(c, _):
        start = pl.multiple_of(c * TK, TK)
        kc = kv_ref[0, pl.ds(start, TK), :].astype(BF16)
        causal = (start + lane) <= rows
        for h in range(N_HEADS):
            s = _nt_dot(q_ref[:, h * MLAQ_SLOT:h * MLAQ_SLOT + MLAQ_DIM], kc) * MLA_SCALE
            s = jnp.where(causal, s, NEG)
            _head_update(h, s, kc[:, :MLA_RANK], m_ref, l_ref, acc_ref)
        return 0

    lax.fori_loop(0, i + 1, body, 0)
    o_ref[...] = _normalised_heads(l_ref, acc_ref)


def mla_prompt(qc, mla_rows, b, t):
    nq = t // TQ
    return pl.pallas_call(
        _mla_prompt_body,
        grid=(b, nq),
        in_specs=[
            pl.BlockSpec((TQ, N_HEADS * MLAQ_SLOT), lambda bi, i: (bi * nq + i, 0)),
            pl.BlockSpec((1, t, MLAQ_DIM), lambda bi, i: (bi, 0, 0)),
        ],
        out_specs=pl.BlockSpec((TQ, N_HEADS * MLA_RANK), lambda bi, i: (bi * nq + i, 0)),
        out_shape=jax.ShapeDtypeStruct((b * t, N_HEADS * MLA_RANK), BF16),
        scratch_shapes=_head_state_scratch(MLAQ_DIM, MLA_RANK)[1:],
        compiler_params=_cparams(2),
        name="mla_prompt",
    )(qc, mla_rows)


def _sortable(score):
    bits = lax.bitcast_convert_type(score + 0.0, I32)
    return bits ^ ((bits >> 31) & jnp.int32(0x7FFFFFFF))


def _ones_where(mask):
    return jnp.where(mask, 1.0, 0.0).astype(F32)


def _kth_largest_key(count_ge, shape):
    def step(it, ut):
        bit = jnp.left_shift(jnp.int32(1), 31 - it)
        cand_u = ut | bit
        cnt = count_ge(cand_u ^ jnp.int32(INT_MIN))
        return jnp.where(cnt >= IDX_TOPK, cand_u, ut)

    ut = lax.fori_loop(0, 32, step, jnp.zeros(shape, I32))
    return ut ^ jnp.int32(INT_MIN)


def _dsa_prompt_body(q_ref, iq_ref, sm_ref, kv_ref, bias_ref, o_ref, key_ref, sel_ref, qs_ref, m_ref, l_ref, acc_ref):
    i = pl.program_id(1)
    nch = i + 1
    rows = i * TQ + lax.broadcasted_iota(I32, (TQ, 1), 0)
    lane = lax.broadcasted_iota(I32, (1, TK), 1)
    iw = sm_ref[:, C_IDXW:C_IDXW + IDX_HEADS] * IDX_W_SCALE

    def score_chunk(c, _):
        start = pl.multiple_of(c * TK, TK)
        ik = kv_ref[0, pl.ds(start, TK), 128:192].astype(BF16)
        score = jnp.zeros((TQ, TK), F32)
        for hh in range(IDX_HEADS):
            d = _nt_dot(iq_ref[:, hh * 64:(hh + 1) * 64].astype(BF16), ik) * IDX_SCALE
            score = score + jnp.maximum(d, 0.0) * iw[:, hh:hh + 1]
        causal = (start + lane) <= rows
        key_ref[c] = jnp.where(causal, _sortable(score), jnp.int32(INT_MIN))
        return 0

    lax.fori_loop(0, nch, score_chunk, 0)

    def count(pred):
        def f(c, acc):
            ones = _ones_where(pred(key_ref[c]))
            return acc + (ones[:, :128] + ones[:, 128:])
        acc = lax.fori_loop(0, nch, f, jnp.zeros((TQ, 128), F32))
        return jnp.sum(acc, axis=-1, keepdims=True)

    thr = _kth_largest_key(lambda cand: count(lambda key: key >= cand), (TQ, 1))
    need = IDX_TOPK - count(lambda key: key > thr)

    r2 = lax.broadcasted_iota(I32, (TK, 2 * TK), 0)
    c2 = lax.broadcasted_iota(I32, (TK, 2 * TK), 1)
    tri_ones = jnp.where((r2 < c2) | (c2 >= TK), 1.0, 0.0).astype(BF16)

    def sel_chunk(c, before):
        start = pl.multiple_of(c * TK, TK)
        key = key_ref[c]
        eq = key == thr
        pre = jnp.dot(jnp.where(eq, 1.0, 0.0).astype(BF16), tri_ones, preferred_element_type=F32)
        prefix = pre[:, :TK] + before
        causal = (start + lane) <= rows
        sel = ((key > thr) | (eq & (prefix < need))) & causal
        sel_ref[c] = jnp.where(sel, 0.0, NEG)
        return before + pre[:, TK:TK + 1]

    lax.fori_loop(0, nch, sel_chunk, jnp.zeros((TQ, 1), F32))

    for h in range(N_HEADS):
        qs_ref[h] = q_ref[:, h * HEAD_DIM:(h + 1) * HEAD_DIM].astype(BF16)
    _init_softmax_state(m_ref, l_ref, acc_ref)

    def body(c, _):
        start = pl.multiple_of(c * TK, TK)
        kv = kv_ref[0, pl.ds(start, TK), 0:128].astype(BF16)
        sel = sel_ref[c]
        far = jnp.minimum(i - c, 2)
        for h in range(N_HEADS):
            s = _nt_dot(qs_ref[h], kv[:, 0:64]) * ATT_SCALE + bias_ref[h, far]
            s = s + sel
            _head_update(h, s, kv[:, 64:128], m_ref, l_ref, acc_ref)
        return 0

    lax.fori_loop(0, nch, body, 0)
    o_ref[...] = _normalised_heads(l_ref, acc_ref)


def dsa_prompt(z, dsa_rows, bias_tab, b, t):
    nq = t // TQ
    return pl.pallas_call(
        _dsa_prompt_body,
        grid=(b, nq),
        in_specs=[
            pl.BlockSpec((TQ, 512), lambda bi, i: (bi * nq + i, C_DSAQ // 512)),
            pl.BlockSpec((TQ, 256), lambda bi, i: (bi * nq + i, C_IDXQ // 256)),
            pl.BlockSpec((TQ, Z_SMALL), lambda bi, i: (bi * nq + i, 0)),
            pl.BlockSpec((1, t, 192), lambda bi, i: (bi, 0, 0)),
            pl.BlockSpec((N_HEADS, 3, TQ, TK), lambda bi, i: (0, 0, 0, 0)),
        ],
        out_specs=pl.BlockSpec((TQ, 512), lambda bi, i: (bi * nq + i, 0)),
        out_shape=jax.ShapeDtypeStruct((b * t, 512), BF16),
        scratch_shapes=[pltpu.VMEM((t // TK, TQ, TK), I32), pltpu.VMEM((t // TK, TQ, TK), F32)]
        + _head_state_scratch(HEAD_DIM, HEAD_DIM),
        compiler_params=_cparams(2),
        name="dsa_prompt",
    )(z, z, z, dsa_rows, bias_tab)


CH = PG * PAGE


def _page_specs(width, n_pages):
    def spec(j):
        return pl.BlockSpec((1, width, PAGE), lambda b, g, pt, j=j: (pt[b * n_pages + g * PG + j], 0, 0))
    return [spec(j) for j in range(PG)]


def _page_specs_rev(width, n_pages):
    ng = n_pages // PG

    def spec(j):
        return pl.BlockSpec((1, width, PAGE), lambda b, g, pt, j=j: (pt[b * n_pages + (ng - 1 - g) * PG + j], 0, 0))
    return [spec(j) for j in range(PG)]


def _online_update_t(s, vt, m, l, acc):
    m_new = jnp.maximum(m, jnp.max(s, axis=-1, keepdims=True))
    alpha = jnp.exp(m - m_new)
    p = jnp.exp(s - m_new)
    l_new = alpha * l + jnp.sum(p, axis=-1, keepdims=True)
    acc_new = alpha * acc + _nt_dot(p.astype(BF16), vt)
    return m_new, l_new, acc_new


def _finish_with_new_row(m, l, acc, s_new, v_new):
    m_tot = jnp.maximum(m, s_new)
    a = jnp.exp(m - m_tot)
    p_new = jnp.exp(s_new - m_tot)
    l_tot = a * l + p_new
    acc_tot = a * acc + p_new.astype(BF16).astype(F32) * v_new.astype(BF16).astype(F32)
    return acc_tot / l_tot


def _fox_sample_body(pt_ref, *refs):
    kv_refs = refs[:PG]
    lf_refs = refs[PG:2 * PG]
    q_ref, kvn_ref, lfn_ref, o_ref, m_ref, l_ref, acc_ref, suf_ref = refs[2 * PG:]
    g = pl.program_id(1)

    @pl.when(g == 0)
    def _():
        m_ref[...] = jnp.full(m_ref.shape, NEG, F32)
        l_ref[...] = jnp.zeros(l_ref.shape, F32)
        acc_ref[...] = jnp.zeros(acc_ref.shape, F32)
        suf_ref[...] = lfn_ref[0]

    q = q_ref[0].astype(BF16)
    r = lax.broadcasted_iota(I32, (PAGE, PAGE), 0)
    c = lax.broadcasted_iota(I32, (PAGE, PAGE), 1)
    later = (r > c).astype(F32)
    suf = suf_ref[...]
    decs = [None] * PG
    for j in reversed(range(PG)):
        lf = lf_refs[j][0]
        decs[j] = jnp.dot(lf, later, precision=HIGHEST, preferred_element_type=F32) + suf
        suf = suf + jnp.sum(lf, axis=1, keepdims=True)
    suf_ref[...] = suf
    decay = jnp.concatenate(decs, axis=1)
    kvt = jnp.concatenate([kv_refs[j][0] for j in range(PG)], axis=1).astype(BF16)
    s = jnp.dot(q, kvt[:128], preferred_element_type=F32) * ATT_SCALE + decay
    m, l, acc = _online_update_t(s, kvt[128:], m_ref[...], l_ref[...], acc_ref[...])
    m_ref[...] = m
    l_ref[...] = l
    acc_ref[...] = acc

    @pl.when(g == pl.num_programs(1) - 1)
    def _():
        kvn = kvn_ref[0]
        qb = q_ref[0].astype(BF16).astype(F32)
        kn = kvn[:, :128].astype(BF16).astype(F32)
        s_new = jnp.sum(qb * kn, axis=-1, keepdims=True) * ATT_SCALE
        o_ref[0] = _finish_with_new_row(m, l, acc, s_new, kvn[:, 128:])


def fox_sample(pt, cache_kv, cache_lf, q8, kv_new, lf_new, n_pages):
    nb = q8.shape[0]
    ng = n_pages // PG
    fixed = lambda b, g, pt: (b, 0, 0)
    grid_spec = pltpu.PrefetchScalarGridSpec(
        num_scalar_prefetch=1,
        grid=(nb, ng),
        in_specs=_page_specs_rev(256, n_pages) + _page_specs_rev(N_HEADS, n_pages) + [
            pl.BlockSpec((1, N_HEADS, 128), fixed),
            pl.BlockSpec((1, 1, 256), fixed),
            pl.BlockSpec((1, N_HEADS, 1), fixed),
        ],
        out_specs=pl.BlockSpec((1, N_HEADS, 128), fixed),
        scratch_shapes=[pltpu.VMEM((N_HEADS, 1), F32), pltpu.VMEM((N_HEADS, 1), F32),
                        pltpu.VMEM((N_HEADS, 128), F32), pltpu.VMEM((N_HEADS, 1), F32)],
    )
    return pl.pallas_call(
        _fox_sample_body,
        grid_spec=grid_spec,
        out_shape=jax.ShapeDtypeStruct((nb, N_HEADS, 128), F32),
        compiler_params=_cparams(2),
        name="fox_sample",
    )(pt, *([cache_kv] * PG), *([cache_lf] * PG), q8, kv_new, lf_new)


def _mla_sample_body(pt_ref, *refs):
    kv_refs = refs[:PG]
    q_ref, kvn_ref, o_ref, m_ref, l_ref, acc_ref = refs[PG:]
    g = pl.program_id(1)

    @pl.when(g == 0)
    def _():
        m_ref[...] = jnp.full(m_ref.shape, NEG, F32)
        l_ref[...] = jnp.zeros(l_ref.shape, F32)
        acc_ref[...] = jnp.zeros(acc_ref.shape, F32)

    q = q_ref[0]
    kct = jnp.concatenate([kv_refs[j][0] for j in range(PG)], axis=1).astype(BF16)
    s = jnp.dot(q, kct, preferred_element_type=F32) * MLA_SCALE
    m, l, acc = _online_update_t(s, kct[:MLA_RANK], m_ref[...], l_ref[...], acc_ref[...])
    m_ref[...] = m
    l_ref[...] = l
    acc_ref[...] = acc

    @pl.when(g == pl.num_programs(1) - 1)
    def _():
        kvn = kvn_ref[0]
        kn = kvn.astype(BF16).astype(F32)
        s_new = jnp.sum(q.astype(F32) * kn, axis=-1, keepdims=True) * MLA_SCALE
        o_ref[0] = _finish_with_new_row(m, l, acc, s_new, kvn[:, :MLA_RANK])


def mla_sample(pt, cache_mla, qc, row_new, n_pages):
    nb = qc.shape[0]
    ng = n_pages // PG
    fixed = lambda b, g, pt: (b, 0, 0)
    grid_spec = pltpu.PrefetchScalarGridSpec(
        num_scalar_prefetch=1,
        grid=(nb, ng),
        in_specs=_page_specs(MLAQ_DIM, n_pages) + [
            pl.BlockSpec((1, N_HEADS, MLAQ_DIM), fixed),
            pl.BlockSpec((1, 1, MLAQ_DIM), fixed),
        ],
        out_specs=pl.BlockSpec((1, N_HEADS, MLA_RANK), fixed),
        scratch_shapes=[pltpu.VMEM((N_HEADS, 1), F32), pltpu.VMEM((N_HEADS, 1), F32),
                        pltpu.VMEM((N_HEADS, MLA_RANK), F32)],
    )
    return pl.pallas_call(
        _mla_sample_body,
        grid_spec=grid_spec,
        out_shape=jax.ShapeDtypeStruct((nb, N_HEADS, MLA_RANK), F32),
        compiler_params=_cparams(2),
        name="mla_sample",
    )(pt, *([cache_mla] * PG), qc, row_new)


def _dsa_sample_body(pt_ref, *refs):
    kv_refs = refs[:PG]
    q_ref, iq_ref, iw_ref, rown_ref, bias_ref, biasn_ref, o_ref, sc_ref, s_ref, v_ref = refs[PG:]
    g = pl.program_id(1)
    ng = pl.num_programs(1)
    kvt = jnp.concatenate([kv_refs[j][0] for j in range(PG)], axis=1).astype(BF16)
    iq = iq_ref[0].astype(BF16)
    iw = iw_ref[0] * IDX_W_SCALE
    d = jnp.dot(iq, kvt[128:192], preferred_element_type=F32) * IDX_SCALE
    score = jnp.maximum(d[0:1], 0.0) * iw[0:1]
    for hh in range(1, IDX_HEADS):
        score = score + jnp.maximum(d[hh:hh + 1], 0.0) * iw[hh:hh + 1]
    sc_ref[pl.ds(g, 1), :] = score
    q = q_ref[0].astype(BF16)
    s_ref[g] = jnp.dot(q, kvt[0:64], preferred_element_type=F32) * ATT_SCALE + bias_ref[g]
    v_ref[g] = kvt[64:128]

    @pl.when(g == ng - 1)
    def _():
        n_groups = sc_ref.shape[0]
        rown = rown_ref[0]
        ikn = rown[:, 128:192].astype(BF16).astype(F32)
        dn = jnp.sum(iq.astype(F32) * ikn, axis=-1, keepdims=True) * IDX_SCALE
        score_n = jnp.maximum(dn[0:1], 0.0) * iw[0:1]
        for hh in range(1, IDX_HEADS):
            score_n = score_n + jnp.maximum(dn[hh:hh + 1], 0.0) * iw[hh:hh + 1]
        key = _sortable(sc_ref[...])
        key_n = _sortable(score_n)

        def count_ge(cand):
            cnt = jnp.sum(jnp.sum(_ones_where(key >= cand), axis=-1, keepdims=True), axis=0, keepdims=True)
            return cnt + _ones_where(key_n >= cand)

        thr = _kth_largest_key(count_ge, (1, 1))
        gt = key > thr
        cnt_gt = (jnp.sum(jnp.sum(_ones_where(gt), axis=-1, keepdims=True), axis=0, keepdims=True)
                  + _ones_where(key_n > thr))
        need = IDX_TOPK - cnt_gt
        eq = key == thr
        eqf = _ones_where(eq)
        r2 = lax.broadcasted_iota(I32, (CH, CH), 0)
        c2 = lax.broadcasted_iota(I32, (CH, CH), 1)
        upper = jnp.where(r2 < c2, 1.0, 0.0).astype(BF16)
        within = jnp.dot(eqf.astype(BF16), upper, preferred_element_type=F32)
        row_tot = jnp.sum(eqf, axis=-1, keepdims=True)
        total_eq = jnp.zeros((1, 1), F32)
        above = []
        for gg in range(n_groups):
            above.append(total_eq)
            total_eq = total_eq + row_tot[gg:gg + 1]
        prefix = within + jnp.concatenate(above, axis=0)
        sel = gt | (eq & (prefix < need))
        sel_n = (key_n > thr) | ((key_n == thr) & (total_eq < need))
        selb = jnp.where(sel, 0.0, NEG)
        selb_n = jnp.where(sel_n, 0.0, NEG)

        kn = rown[:, 0:64].astype(BF16).astype(F32)
        s_new = jnp.sum(q.astype(F32) * kn, axis=-1, keepdims=True) * ATT_SCALE + biasn_ref[...] + selb_n
        m = s_new
        rows = []
        for gg in range(n_groups):
            sg = s_ref[gg] + selb[gg:gg + 1, :]
            rows.append(sg)
            m = jnp.maximum(m, jnp.max(sg, axis=-1, keepdims=True))
        p_new = jnp.exp(s_new - m)
        l = p_new
        acc = p_new.astype(BF16).astype(F32) * rown[:, 64:128].astype(BF16).astype(F32)
        for gg in range(n_groups):
            p = jnp.exp(rows[gg] - m)
            l = l + jnp.sum(p, axis=-1, keepdims=True)
            acc = acc + _nt_dot(p.astype(BF16), v_ref[gg])
        o_ref[0] = acc / l


def dsa_sample(pt, cache_dsa, q, iq, iw, row_new, bias_s, bias_n, n_pages):
    nb = q.shape[0]
    ng = n_pages // PG
    fixed = lambda b, g, pt: (b, 0, 0)
    grid_spec = pltpu.PrefetchScalarGridSpec(
        num_scalar_prefetch=1,
        grid=(nb, ng),
        in_specs=_page_specs(192, n_pages) + [
            pl.BlockSpec((1, N_HEADS, HEAD_DIM), fixed),
            pl.BlockSpec((1, IDX_HEADS, 64), fixed),
            pl.BlockSpec((1, IDX_HEADS, 1), fixed),
            pl.BlockSpec((1, 1, 192), fixed),
            pl.BlockSpec((ng, N_HEADS, CH), lambda b, g, pt: (0, 0, 0)),
            pl.BlockSpec((N_HEADS, 1), lambda b, g, pt: (0, 0)),
        ],
        out_specs=pl.BlockSpec((1, N_HEADS, HEAD_DIM), fixed),
        scratch_shapes=[pltpu.VMEM((ng, CH), F32), pltpu.VMEM((ng, N_HEADS, CH), F32),
                        pltpu.VMEM((ng, HEAD_DIM, CH), BF16)],
    )
    return pl.pallas_call(
        _dsa_sample_body,
        grid_spec=grid_spec,
        out_shape=jax.ShapeDtypeStruct((nb, N_HEADS, HEAD_DIM), F32),
        compiler_params=_cparams(2),
        name="dsa_sample",
    )(pt, *([cache_dsa] * PG), q, iq, iw, row_new, bias_s, bias_n)


def _t5_bucket(dist):
    max_exact = NUM_BUCKETS // 2
    d = jnp.maximum(dist, 0)
    log_ratio = jnp.log(jnp.maximum(d, max_exact).astype(F32) / max_exact) / math.log(MAX_DISTANCE / max_exact)
    large = jnp.minimum(max_exact + (log_ratio * (NUM_BUCKETS - max_exact)).astype(I32), NUM_BUCKETS - 1)
    return jnp.where(d < max_exact, d, large)


def _rope_tables(pos, reps):
    half = MLA_ROPE // 2
    inv = ROPE_BASE ** (-jnp.arange(half, dtype=F32) / half)
    ang = pos.astype(F32)[:, None] * inv[None, :]
    return jnp.tile(jnp.cos(ang), (1, reps)), jnp.tile(jnp.sin(ang), (1, reps))


def _prep_layer_weights(w_in, w_mla_uk, w_mla_uv, w_branch_fox, w_branch_dsa, w_branch_mla, w_out, w_up, w_down):
    depth, d, _ = w_in.shape
    off = {}
    o = 0
    for name, width in (('fox_q', 512), ('fox_k', 128), ('fox_v', 128), ('fox_f', 8), ('dsa_q', 512), ('dsa_k', 64),
                        ('dsa_v', 64), ('idx_q', 256), ('idx_k', 64), ('idx_w', 4), ('mla_q', 768), ('mla_ckv', 128),
                        ('mla_kr', 32), ('gates', 3072)):
        off[name] = (o, width)
        o += width

    def seg(name):
        a, w = off[name]
        return w_in[:, :, a:a + w]

    def pad(w):
        return jnp.zeros((depth, d, w), w_in.dtype)

    mq = seg('mla_q').reshape(depth, d, N_HEADS, MLA_NOPE + MLA_ROPE)
    half = MLA_ROPE // 2
    mla_q = jnp.concatenate([
        mq[..., :MLA_NOPE].reshape(depth, d, N_HEADS * MLA_NOPE),
        mq[..., MLA_NOPE:MLA_NOPE + half].reshape(depth, d, N_HEADS * half),
        mq[..., MLA_NOPE + half:].reshape(depth, d, N_HEADS * half)], axis=-1)
    cols = [seg('fox_k'), seg('fox_v'), seg('dsa_k'), seg('dsa_v'), seg('idx_k'), pad(64),
            seg('mla_ckv'), seg('mla_kr'), seg('fox_f'), seg('idx_w'), pad(Z_SMALL - 684),
            mla_q, seg('fox_q'), seg('dsa_q'), seg('idx_q'), pad(C_GATES - C_IDXQ - 256), seg('gates')]
    wz = jnp.concatenate(cols, axis=-1).astype(BF16)
    assert wz.shape[-1] == Z_WIDTH
    return dict(
        wz=wz,
        wuk=jnp.transpose(w_mla_uk, (0, 2, 3, 1)).astype(BF16),
        wuv=jnp.transpose(w_mla_uv, (0, 2, 1, 3)).astype(BF16),
        wbf=w_branch_fox.astype(BF16), wbd=w_branch_dsa.astype(BF16), wbm=w_branch_mla.astype(BF16),
        wo=w_out.astype(BF16), wup=w_up.astype(BF16), wdn=w_down.astype(BF16),
    )


def _prompt_bias_tiles(rel_bias):
    r = jnp.arange(TQ)[:, None]
    c = jnp.arange(TK)[None, :]
    d = jnp.stack([jnp.maximum(r - c, 0), r - c + TK, jnp.full((TQ, TK), 2 * TK, I32)], axis=0)
    return jnp.transpose(rel_bias[_t5_bucket(d)], (3, 0, 1, 2))


def _trunk_prompt(x, lw, g_attn, b_forget, g_mla_kv, g_mlp, rel_bias, g_final):
    b, t, d = x.shape
    n = b * t
    x = x.reshape(n, d)
    pos = jnp.arange(t, dtype=I32)
    cos32, sin_h = _rope_tables(pos, 1)
    cos32 = jnp.concatenate([cos32, cos32], axis=1)
    sin32 = jnp.concatenate([-sin_h, sin_h], axis=1)
    cos128, sin128 = _rope_tables(pos, N_HEADS)
    bias_tab = _prompt_bias_tiles(rel_bias)
    rows = ([], [], [], [])
    tm = 1024
    for l in range(DEPTH):
        z = rms_matmul(x, g_attn[l][None], lw['wz'][l], tm, 1024)
        foxkv, logf, dsa, mla = post_rows(z, g_mla_kv[l][None], b_forget[l][None], cos32, sin32, tm)
        cum, cumt = fox_cumsum(logf.reshape(b, t, N_HEADS))
        o_fox = fox_prompt(z, foxkv.reshape(b, t, 256), cum, cumt, b, t)
        o_dsa = dsa_prompt(z, dsa.reshape(b, t, 192), bias_tab, b, t)
        qc = mla_q_prep(z, cos128, sin128, lw['wuk'][l], tm)
        o_lat = mla_prompt(qc, mla.reshape(b, t, MLAQ_DIM), b, t)
        o_mla = mla_uv(o_lat, lw['wuv'][l], tm)
        x = merge_out(o_fox, o_dsa, o_mla, z, lw['wbf'][l], lw['wbd'][l], lw['wbm'][l], lw['wo'][l], x, 512)
        x = mlp_residual(x, g_mlp[l][None], lw['wup'][l], lw['wdn'][l], tm, 1024)
        rows[0].append(foxkv.reshape(b, t, 2, FOX_KV_HEADS, HEAD_DIM))
        rows[1].append(logf.reshape(b, t, N_HEADS))
        rows[2].append(dsa.reshape(b, t, 3, HEAD_DIM))
        rows[3].append(mla.reshape(b, t, MLAQ_DIM))
    y = rms_only(x, g_final[None], tm).reshape(b, t, d)
    return y, tuple(jnp.stack(r, axis=0) for r in rows)


def _trunk_sample(x, caches, page_table, lw, g_attn, b_forget, g_mla_kv, g_mlp, rel_bias, g_final):
    cache_fox_kv, cache_fox_logf, cache_dsa_kv, cache_mla = caches
    nb, t, d = x.shape
    n_pages = page_table.shape[1]
    past = n_pages * PAGE
    ng = n_pages // PG
    n_pool = cache_fox_kv.shape[1]
    x = x.reshape(nb, d)
    pos = jnp.full((nb,), past, I32)
    cos32, sin_h = _rope_tables(pos, 1)
    cos32 = jnp.concatenate([cos32, cos32], axis=1)
    sin32 = jnp.concatenate([-sin_h, sin_h], axis=1)
    cos128, sin128 = _rope_tables(pos, N_HEADS)
    by_dist = rel_bias[_t5_bucket(jnp.arange(past + 1))]
    bias_s = jnp.transpose(by_dist[past - jnp.arange(past)].reshape(ng, CH, N_HEADS), (0, 2, 1))
    bias_n = by_dist[0][:, None]
    c_fox = jnp.transpose(cache_fox_kv, (0, 1, 3, 4, 5, 2)).reshape(DEPTH * n_pool, 256, PAGE)
    c_lf = jnp.transpose(cache_fox_logf, (0, 1, 3, 2)).reshape(DEPTH * n_pool, N_HEADS, PAGE)
    c_dsa = jnp.transpose(cache_dsa_kv, (0, 1, 3, 4, 2)).reshape(DEPTH * n_pool, 192, PAGE)
    c_mla = jnp.transpose(cache_mla, (0, 1, 3, 2)).reshape(DEPTH * n_pool, MLAQ_DIM, PAGE)
    group = N_HEADS // FOX_KV_HEADS
    half_mask = (jnp.arange(N_HEADS)[:, None] // group == jnp.arange(128)[None, :] // HEAD_DIM)
    rows = ([], [], [], [])
    tm = nb
    for l in range(DEPTH):
        pt = (page_table + l * n_pool).reshape(-1).astype(I32)
        z = rms_matmul(x, g_attn[l][None], lw['wz'][l], tm, 1024)
        foxkv, logf, dsa, mla = post_rows(z, g_mla_kv[l][None], b_forget[l][None], cos32, sin32, tm)
        fq = z[:, C_FOXQ:C_FOXQ + 512].reshape(nb, N_HEADS, HEAD_DIM)
        q8 = jnp.where(half_mask[None], jnp.concatenate([fq, fq], axis=-1), 0.0)
        o8 = fox_sample(pt, c_fox, c_lf, q8, foxkv[:, None, :], logf[:, :, None], n_pages)
        o8 = o8.reshape(nb, N_HEADS, FOX_KV_HEADS, HEAD_DIM)
        o_fox = jnp.concatenate([o8[:, :group, 0], o8[:, group:, 1]], axis=1).reshape(nb, 512).astype(BF16)
        dq = z[:, C_DSAQ:C_DSAQ + 512].reshape(nb, N_HEADS, HEAD_DIM)
        iq = z[:, C_IDXQ:C_IDXQ + 256].reshape(nb, IDX_HEADS, 64)
        iw = z[:, C_IDXW:C_IDXW + IDX_HEADS].reshape(nb, IDX_HEADS, 1)
        o_dsa = dsa_sample(pt, c_dsa, dq, iq, iw, dsa[:, None, :], bias_s, bias_n, n_pages)
        o_dsa = o_dsa.reshape(nb, 512).astype(BF16)
        qc = mla_q_prep(z, cos128, sin128, lw['wuk'][l], tm)
        qc = qc.reshape(nb, N_HEADS, MLAQ_SLOT)[:, :, :MLAQ_DIM]
        o_lat = mla_sample(pt, c_mla, qc, mla[:, None, :], n_pages)
        o_mla = mla_uv(o_lat.reshape(nb, N_HEADS * MLA_RANK).astype(BF16), lw['wuv'][l], tm)
        x = merge_out(o_fox, o_dsa, o_mla, z, lw['wbf'][l], lw['wbd'][l], lw['wbm'][l], lw['wo'][l], x, tm)
        x = mlp_residual(x, g_mlp[l][None], lw['wup'][l], lw['wdn'][l], tm, 1024)
        rows[0].append(foxkv.reshape(nb, 1, 2, FOX_KV_HEADS, HEAD_DIM))
        rows[1].append(logf.reshape(nb, 1, N_HEADS))
        rows[2].append(dsa.reshape(nb, 1, 3, HEAD_DIM))
        rows[3].append(mla.reshape(nb, 1, MLAQ_DIM))
    y = rms_only(x, g_final[None], tm).reshape(nb, 1, d)
    return y, tuple(jnp.stack(r, axis=0) for r in rows)


def kernel(x_prompt, x_sample, cache_fox_kv, cache_fox_logf, cache_dsa_kv, cache_mla, page_table, g_attn, w_in,
           b_forget, g_mla_kv, w_mla_uk, w_mla_uv, w_branch_fox, w_branch_dsa, w_branch_mla, w_out, g_mlp, w_up,
           w_down, rel_bias, g_final):
    lw = _prep_layer_weights(w_in, w_mla_uk, w_mla_uv, w_branch_fox, w_branch_dsa, w_branch_mla, w_out, w_up, w_down)
    y_p, (fox_kv_p, fox_logf_p, dsa_kv_p, mla_p) = _trunk_prompt(
        x_prompt, lw, g_attn, b_forget, g_mla_kv, g_mlp, rel_bias, g_final)
    y_s, (fox_kv_s, fox_logf_s, dsa_kv_s, mla_s) = _trunk_sample(
        x_sample, (cache_fox_kv, cache_fox_logf, cache_dsa_kv, cache_mla), page_table, lw, g_attn, b_forget,
        g_mla_kv, g_mlp, rel_bias, g_final)
    return (y_p, y_s, fox_kv_p, fox_logf_p, dsa_kv_p, mla_p, fox_kv_s, fox_logf_s, dsa_kv_s, mla_s)
```
